```python
import math
import jax
import jax.numpy as jnp
from jax import lax
import numpy as np

D_MODEL = 1024
BATCH = 8
SEQ = 2048
DEPTH = 4
DEC_BATCH = 128
DEC_SEQ = 4
PAST_LEN = 2048
PAGE_SIZE = 128

MOBA_HEADS = 8
MOBA_HEAD_DIM = 64
MOBA_WIDTH = MOBA_HEADS * MOBA_HEAD_DIM
MOBA_BLOCK = 256
MOBA_TOPK = 3
MOBA_ROWS = 64
ROPE_THETA = 10000.0
GDN_HEADS = 4
GDN_DK = 128
GDN_DV = 128
GDN_KEY_WIDTH = GDN_HEADS * GDN_DK
GDN_WIDTH = GDN_HEADS * GDN_DV
GDN_CONV_DIM = 2 * GDN_KEY_WIDTH + GDN_WIDTH
GDN_CHUNK = 64
CONV_WIDTH = 4
LRU_WIDTH = D_MODEL // 2
LRU_HEADS = 8
LRU_BLOCK = LRU_WIDTH // LRU_HEADS
LRU_C = 8.0
D_FF = 4 * D_MODEL
PLE_DIM = 256
N_BRANCH = 3
EPS = 1e-6
IN_SIZES = (3 * MOBA_WIDTH, GDN_CONV_DIM, GDN_WIDTH, GDN_HEADS, GDN_HEADS, LRU_WIDTH, LRU_WIDTH, N_BRANCH * D_MODEL)
N_IN = 3 * MOBA_WIDTH + GDN_CONV_DIM + GDN_WIDTH + 2 * GDN_HEADS + 2 * LRU_WIDTH + N_BRANCH * D_MODEL

kernel_name = 'moba_gdn_rglru_parallel_hybrid_step'


def rms_norm(x, gain):
    xf = x.astype(jnp.float32)
    y = xf * lax.rsqrt(jnp.mean(xf * xf, axis=-1, keepdims=True) + EPS)
    return (y * gain.astype(jnp.float32)).astype(x.dtype)


def l2_normalize(x):
    xf = x.astype(jnp.float32)
    return xf * lax.rsqrt(jnp.sum(xf * xf, axis=-1, keepdims=True) + EPS)


def rotary(x, pos):
    half = x.shape[-1] // 2
    inv_freq = ROPE_THETA ** (-jnp.arange(half, dtype=jnp.float32) / half)
    ang = pos.astype(jnp.float32)[:, None] * inv_freq[None, :]
    cos = jnp.cos(ang)[None, :, None, :]
    sin = jnp.sin(ang)[None, :, None, :]
    xf = x.astype(jnp.float32)
    x1, x2 = xf[..., :half], xf[..., half:]
    return jnp.concatenate([x1 * cos - x2 * sin, x2 * cos + x1 * sin], axis=-1).astype(x.dtype)


def causal_depthwise_conv(x, buf, w):
    xx = jnp.concatenate([buf.astype(x.dtype), x], axis=1)
    y = lax.conv_general_dilated(xx, w[:, None, :].astype(x.dtype), window_strides=(1,), padding='VALID',
                                 dimension_numbers=('NWC', 'WIO', 'NWC'), feature_group_count=x.shape[-1])
    return y, xx[:, xx.shape[1] - (CONV_WIDTH - 1):]


def split_in_proj(u):
    parts, start = [], 0
    for size in IN_SIZES:
        parts.append(u[..., start:start + size])
        start += size
    return parts


def moba_attention(q, k, v, q_pos0):
    B, T, H, Dh = q.shape
    Lk = k.shape[1]
    n_blk = -(-Lk // MOBA_BLOCK)
    pad = n_blk * MOBA_BLOCK - Lk
    kp = jnp.pad(k, ((0, 0), (0, pad), (0, 0), (0, 0))).reshape(B, n_blk, MOBA_BLOCK, H, Dh)
    vp = jnp.pad(v, ((0, 0), (0, pad), (0, 0), (0, 0))).reshape(B, n_blk, MOBA_BLOCK, H, Dh)
    k_mean = jnp.mean(kp.astype(jnp.float32), axis=2)
    pos = q_pos0 + jnp.arange(T, dtype=jnp.int32)
    own = pos // MOBA_BLOCK
    gate = jnp.einsum('bthd,bnhd->bthn', q.astype(jnp.float32), k_mean)
    fully_past = jnp.arange(n_blk, dtype=jnp.int32)[None, :] < own[:, None]
    gate = jnp.where(fully_past[None, :, None, :], gate, -jnp.inf)
    n_sel = min(MOBA_TOPK, n_blk)
    top_val, top_idx = lax.top_k(gate, n_sel)
    own_b = jnp.broadcast_to(own[None, :, None, None], (B, T, H, 1))
    blk_idx = jnp.concatenate([top_idx.astype(jnp.int32), own_b], axis=-1)
    blk_ok = jnp.concatenate([jnp.isfinite(top_val).astype(jnp.int32), jnp.ones((B, T, H, 1), jnp.int32)], axis=-1)
    n_slot = n_sel + 1
    R = B * T
    n_grp = -(-R // MOBA_ROWS)
    rpad = n_grp * MOBA_ROWS - R

    def to_rows(a, fill):
        a = a.reshape((R,) + a.shape[2:])
        a = jnp.pad(a, [(0, rpad)] + [(0, 0)] * (a.ndim - 1), constant_values=fill)
        return a.reshape((n_grp, MOBA_ROWS) + a.shape[1:])

    q_r = to_rows(q, 0)
    b_r = to_rows(jnp.broadcast_to(jnp.arange(B, dtype=jnp.int32)[:, None], (B, T)), 0)
    pos_r = to_rows(jnp.broadcast_to(pos[None, :], (B, T)), 0)
    idx_r = to_rows(blk_idx, 0)
    ok_r = to_rows(blk_ok, 1)
    k_blk = kp.reshape(B * n_blk, MOBA_BLOCK, H, Dh)
    v_blk = vp.reshape(B * n_blk, MOBA_BLOCK, H, Dh)
    head = jnp.arange(H, dtype=jnp.int32)[None, :, None, None]
    blk_rows = jnp.arange(MOBA_BLOCK, dtype=jnp.int32)
    scale = Dh ** -0.5

    def attend(args):
        qg, bg, pg, ig, og = args
        src = (bg[:, None, None] * n_blk + ig)[..., None]
        kg = k_blk[src, blk_rows, head].astype(jnp.float32)
        vg = v_blk[src, blk_rows, head].astype(jnp.float32)
        s = jnp.einsum('ghd,ghsjd->ghsj', qg.astype(jnp.float32), kg) * scale
        key_pos = ig[..., None] * MOBA_BLOCK + blk_rows
        mask = (og[..., None] > 0) & (key_pos <= pg[:, None, None, None])
        s = jnp.where(mask, s, -jnp.inf).reshape(MOBA_ROWS, H, n_slot * MOBA_BLOCK)
        pr = jax.nn.softmax(s, axis=-1).reshape(MOBA_ROWS, H, n_slot, MOBA_BLOCK)
        return jnp.einsum('ghsj,ghsjd->ghd', pr, vg)

    out = lax.map(attend, (q_r, b_r, pos_r, idx_r, ok_r))
    return out.reshape(n_grp * MOBA_ROWS, H, Dh)[:R].reshape(B, T, H, Dh).astype(q.dtype)


def gated_delta_rule(q, k, v, g, beta, s0):
    B, T, H, DK = q.shape
    DV = v.shape[-1]
    C = GDN_CHUNK
    N = -(-T // C)
    pad = N * C - T

    def chunks(a):
        a = a.astype(jnp.float32)
        a = jnp.pad(a, [(0, 0), (0, pad)] + [(0, 0)] * (a.ndim - 2))
        a = a.reshape((B, N, C) + a.shape[2:])
        return jnp.moveaxis(a, (1, 3), (0, 2))

    qc, kc, vc, gc, bc = chunks(q), chunks(k), chunks(v), chunks(g), chunks(beta)
    gc = jnp.cumsum(gc, axis=-1)
    tri = jnp.tril(jnp.ones((C, C), bool))
    strict = jnp.tril(jnp.ones((C, C), bool), -1)
    decay = jnp.exp(jnp.where(tri, gc[..., :, None] - gc[..., None, :], -jnp.inf))
    kb = kc * bc[..., None]
    low = jnp.where(strict, jnp.einsum('nbhid,nbhjd->nbhij', kb, kc) * decay, 0.0)
    eye = jnp.eye(C, dtype=jnp.float32)
    t_mat = lax.linalg.triangular_solve(eye + low, jnp.broadcast_to(eye, low.shape),
                                        left_side=True, lower=True, unit_diagonal=True)
    u = t_mat @ (vc * bc[..., None])
    w = t_mat @ (kb * jnp.exp(gc)[..., None])
    a_intra = jnp.einsum('nbhid,nbhjd->nbhij', qc, kc) * decay
    q_dec = qc * jnp.exp(gc)[..., None]
    k_dec = kc * jnp.exp(gc[..., -1:] - gc)[..., None]
    g_last = jnp.exp(gc[..., -1])

    def step(S, xs):
        qd, kd, u_c, w_c, a_c, gl = xs
        v_new = u_c - w_c @ S
        o = qd @ S + a_c @ v_new
        S = S * gl[..., None, None] + jnp.einsum('bhck,bhcv->bhkv', kd, v_new)
        return S, o

    S, o = lax.scan(step, s0.astype(jnp.float32), (q_dec, k_dec, u, w, a_intra, g_last))
    o = jnp.moveaxis(o, (0, 2), (1, 3)).reshape(B, N * C, H, DV)[:, :T]
    return o, S


def gated_deltanet(qkv, z, a, b, conv_buf, s0, lw):
    B, T, _ = qkv.shape
    y, conv_new = causal_depthwise_conv(qkv, conv_buf, lw['gdn_conv_w'])
    y = jax.nn.silu(y.astype(jnp.float32))
    q = l2_normalize(y[..., :GDN_KEY_WIDTH].reshape(B, T, GDN_HEADS, GDN_DK)) * (GDN_DK ** -0.5)
    k = l2_normalize(y[..., GDN_KEY_WIDTH:2 * GDN_KEY_WIDTH].reshape(B, T, GDN_HEADS, GDN_DK))
    v = y[..., 2 * GDN_KEY_WIDTH:].reshape(B, T, GDN_HEADS, GDN_DV)
    beta = jax.nn.sigmoid(b.astype(jnp.float32))
    g = -jnp.exp(lw['gdn_a_log'].astype(jnp.float32)) * jax.nn.softplus(a.astype(jnp.float32) + lw['gdn_dt_bias'].astype(jnp.float32))
    o, s_new = gated_delta_rule(q, k, v, g, beta, s0)
    o = rms_norm(o, lw['gdn_out_norm']) * jax.nn.silu(z.astype(jnp.float32).reshape(B, T, GDN_HEADS, GDN_DV))
    return o.reshape(B, T, GDN_WIDTH).astype(qkv.dtype), conv_new, s_new.astype(s0.dtype)


def rg_lru_block(x_in, y_in, conv_buf, h0, lw):
    B, T, _ = x_in.shape
    xc, conv_new = causal_depthwise_conv(x_in, conv_buf, lw['lru_conv_w'])
    xf = xc.astype(jnp.float32) + lw['lru_conv_b'].astype(jnp.float32)
    xh = xf.reshape(B, T, LRU_HEADS, LRU_BLOCK)
    r = jax.nn.sigmoid(jnp.einsum('bthi,hij->bthj', xh, lw['lru_wa'].astype(jnp.float32)).reshape(B, T, LRU_WIDTH) + lw['lru_ba'].astype(jnp.float32))
    i = jax.nn.sigmoid(jnp.einsum('bthi,hij->bthj', xh, lw['lru_wx'].astype(jnp.float32)).reshape(B, T, LRU_WIDTH) + lw['lru_bx'].astype(jnp.float32))
    log_a = -LRU_C * r * jax.nn.softplus(-lw['lru_lambda'].astype(jnp.float32))
    a = jnp.exp(log_a)
    bt = jnp.sqrt(-jnp.expm1(2.0 * log_a)) * (i * xf)
    bt = bt.at[:, 0].add(a[:, 0] * h0.astype(jnp.float32))

    def combine(left, right):
        a_l, b_l = left
        a_r, b_r = right
        return a_l * a_r, a_r * b_l + b_r

    _, h = lax.associative_scan(combine, (a, bt), axis=1)
    out = h * jax.nn.gelu(y_in.astype(jnp.float32))
    return out.astype(x_in.dtype), conv_new, h[:, -1].astype(h0.dtype)


def trunk_layer(x, p, pos0, past_k, past_v, gdn_conv_buf, gdn_s0, lru_conv_buf, lru_h0, lw):
    B, T, _ = x.shape
    h = rms_norm(x, lw['g_mix'])
    u = h @ lw['w_in']
    qkv_a, qkv_d, z_d, a_d, b_d, x_r, y_r, gate_logits = split_in_proj(u)
    pos = pos0 + jnp.arange(T, dtype=jnp.int32)
    q_a, k_a, v_a = [t.reshape(B, T, MOBA_HEADS, MOBA_HEAD_DIM) for t in jnp.split(qkv_a, 3, axis=-1)]
    q_a = rotary(rms_norm(q_a, lw['moba_q_norm']), pos)
    k_a = rotary(rms_norm(k_a, lw['moba_k_norm']), pos)
    if past_k is None:
        k_all, v_all = k_a, v_a
    else:
        k_all = jnp.concatenate([past_k.astype(k_a.dtype), k_a], axis=1)
        v_all = jnp.concatenate([past_v.astype(v_a.dtype), v_a], axis=1)
    o_a = moba_attention(q_a, k_all, v_all, pos0).reshape(B, T, MOBA_WIDTH)
    o_d, gdn_conv_new, gdn_s_new = gated_deltanet(qkv_d, z_d, a_d, b_d, gdn_conv_buf, gdn_s0, lw)
    o_r, lru_conv_new, lru_h_new = rg_lru_block(x_r, y_r, lru_conv_buf, lru_h0, lw)
    g_a, g_d, g_r = jnp.split(jax.nn.sigmoid(gate_logits), N_BRANCH, axis=-1)
    merged = g_a * (o_a @ lw['w_branch_a']) + g_d * (o_d @ lw['w_branch_d']) + g_r * (o_r @ lw['w_branch_r'])
    x = x + merged @ lw['w_out']
    h = rms_norm(x, lw['g_ffn'])
    x = x + jnp.square(jax.nn.relu(h @ lw['w_up'])) @ lw['w_down']
    x = x + jax.nn.sigmoid(rms_norm(x, lw['g_ple']) @ lw['w_ple_gate']) * (p @ lw['w_ple_proj'])
    return x, k_a, v_a, gdn_conv_new, gdn_s_new, lru_conv_new, lru_h_new


def setup_inputs(seed: int = 0) -> dict:
    key = jax.random.key(seed)
    keys = jax.random.split(key, 64)
    counter = [0]

    def nk():
        counter[0] += 1
        return keys[counter[0] - 1]

    def normal(shape, scale):
        return jax.random.normal(nk(), shape, jnp.float32) * scale

    def gain(shape):
        return 1.0 + normal(shape, 0.02)

    n_pages = PAST_LEN // PAGE_SIZE
    n_used = DEC_BATCH * n_pages
    n_pool = n_used + n_used // 4
    page_table = jax.random.permutation(nk(), n_pool)[:n_used].reshape(DEC_BATCH, n_pages).astype(jnp.int32)

    dt = jnp.exp(jax.random.uniform(nk(), (DEPTH, GDN_HEADS), jnp.float32, math.log(1e-3), math.log(1e-1)))
    gdn_dt_bias = dt + jnp.log(-jnp.expm1(-dt))
    gdn_a_log = jnp.log(jax.random.uniform(nk(), (DEPTH, GDN_HEADS), jnp.float32, 1.0, 16.0))
    a_pow = jax.random.uniform(nk(), (DEPTH, LRU_WIDTH), jnp.float32, 0.9, 0.999) ** (1.0 / LRU_C)
    lru_lambda = jnp.log(a_pow) - jnp.log1p(-a_pow)

    return {
        'x_prompt': normal((BATCH, SEQ, D_MODEL), 1.0),
        'x_sample': normal((DEC_BATCH, DEC_SEQ, D_MODEL), 1.0),
        'cache_k': normal((n_pool, PAGE_SIZE, DEPTH, MOBA_HEADS, MOBA_HEAD_DIM), 1.0),
        'cache_v': normal((n_pool, PAGE_SIZE, DEPTH, MOBA_HEADS, MOBA_HEAD_DIM), 1.0),
        'state_gdn': normal((DEC_BATCH, DEPTH, GDN_HEADS, GDN_DK, GDN_DV), 0.1),
        'state_gdn_conv': normal((DEC_BATCH, DEPTH, CONV_WIDTH - 1, GDN_CONV_DIM), 1.0),
        'state_lru_h': normal((DEC_BATCH, DEPTH, LRU_WIDTH), 0.5),
        'state_lru_conv': normal((DEC_BATCH, DEPTH, CONV_WIDTH - 1, LRU_WIDTH), 1.0),
        'page_table': page_table,
        'p_prompt': normal((DEPTH, BATCH, SEQ, PLE_DIM), 1.0),
        'p_sample': normal((DEPTH, DEC_BATCH, DEC_SEQ, PLE_DIM), 1.0),
        'g_mix': gain((DEPTH, D_MODEL)),
        'w_in': normal((DEPTH, D_MODEL, N_IN), D_MODEL ** -0.5),
        'moba_q_norm': gain((DEPTH, MOBA_HEAD_DIM)),
        'moba_k_norm': gain((DEPTH, MOBA_HEAD_DIM)),
        'w_branch_a': normal((DEPTH, MOBA_WIDTH, D_MODEL), MOBA_WIDTH ** -0.5),
        'gdn_conv_w': normal((DEPTH, CONV_WIDTH, GDN_CONV_DIM), CONV_WIDTH ** -0.5),
        'gdn_a_log': gdn_a_log,
        'gdn_dt_bias': gdn_dt_bias,
        'gdn_out_norm': gain((DEPTH, GDN_DV)),
        'w_branch_d': normal((DEPTH, GDN_WIDTH, D_MODEL), GDN_WIDTH ** -0.5),
        'lru_conv_w': normal((DEPTH, CONV_WIDTH, LRU_WIDTH), CONV_WIDTH ** -0.5),
        'lru_conv_b': normal((DEPTH, LRU_WIDTH), 0.01),
        'lru_wa': normal((DEPTH, LRU_HEADS, LRU_BLOCK, LRU_BLOCK), LRU_BLOCK ** -0.5),
        'lru_ba': normal((DEPTH, LRU_WIDTH), 0.01),
        'lru_wx': normal((DEPTH, LRU_HEADS, LRU_BLOCK, LRU_BLOCK), LRU_BLOCK ** -0.5),
        'lru_bx': normal((DEPTH, LRU_WIDTH), 0.01),
        'lru_lambda': lru_lambda,
        'w_branch_r': normal((DEPTH, LRU_WIDTH, D_MODEL), LRU_WIDTH ** -0.5),
        'w_out': normal((DEPTH, D_MODEL, D_MODEL), D_MODEL ** -0.5),
        'g_ffn': gain((DEPTH, D_MODEL)),
        'w_up': normal((DEPTH, D_MODEL, D_FF), D_MODEL ** -0.5),
        'w_down': normal((DEPTH, D_FF, D_MODEL), D_FF ** -0.5),
        'g_ple': gain((DEPTH, D_MODEL)),
        'w_ple_gate': normal((DEPTH, D_MODEL, D_MODEL), D_MODEL ** -0.5),
        'w_ple_proj': normal((DEPTH, PLE_DIM, D_MODEL), PLE_DIM ** -0.5),
    }


def reference(x_prompt, x_sample, cache_k, cache_v, state_gdn, state_gdn_conv, state_lru_h, state_lru_conv,
              page_table, p_prompt, p_sample, g_mix, w_in, moba_q_norm, moba_k_norm, w_branch_a, gdn_conv_w,
              gdn_a_log, gdn_dt_bias, gdn_out_norm, w_branch_d, lru_conv_w, lru_conv_b, lru_wa, lru_ba, lru_wx,
              lru_bx, lru_lambda, w_branch_r, w_out, g_ffn, w_up, w_down, g_ple, w_ple_gate, w_ple_proj):
    bp = x_prompt.shape[0]
    bs = x_sample.shape[0]
    past_len = page_table.shape[1] * PAGE_SIZE
    page_rows = jnp.arange(PAGE_SIZE, dtype=jnp.int32)[None, None, :]
    zero_gdn_conv = jnp.zeros((bp, CONV_WIDTH - 1, GDN_CONV_DIM), x_prompt.dtype)
    zero_gdn_s = jnp.zeros((bp, GDN_HEADS, GDN_DK, GDN_DV), x_prompt.dtype)
    zero_lru_conv = jnp.zeros((bp, CONV_WIDTH - 1, LRU_WIDTH), x_prompt.dtype)
    zero_lru_h = jnp.zeros((bp, LRU_WIDTH), x_prompt.dtype)
    yp, ys = x_prompt, x_sample
    kp_l, vp_l, ks_l, vs_l = [], [], [], []
    gsp_l, gss_l, gcp_l, gcs_l = [], [], [], []
    lhp_l, lhs_l, lcp_l, lcs_l = [], [], [], []
    for l in range(DEPTH):
        lw = {'g_mix': g_mix[l], 'w_in': w_in[l], 'moba_q_norm': moba_q_norm[l], 'moba_k_norm': moba_k_norm[l],
              'w_branch_a': w_branch_a[l], 'gdn_conv_w': gdn_conv_w[l], 'gdn_a_log': gdn_a_log[l],
              'gdn_dt_bias': gdn_dt_bias[l], 'gdn_out_norm': gdn_out_norm[l], 'w_branch_d': w_branch_d[l],
              'lru_conv_w': lru_conv_w[l], 'lru_conv_b': lru_conv_b[l], 'lru_wa': lru_wa[l], 'lru_ba': lru_ba[l],
              'lru_wx': lru_wx[l], 'lru_bx': lru_bx[l], 'lru_lambda': lru_lambda[l], 'w_branch_r': w_branch_r[l],
              'w_out': w_out[l], 'g_ffn': g_ffn[l], 'w_up': w_up[l], 'w_down': w_down[l], 'g_ple': g_ple[l],
              'w_ple_gate': w_ple_gate[l], 'w_ple_proj': w_ple_proj[l]}
        yp, k_new, v_new, gconv, gstate, lconv, lh = trunk_layer(
            yp, p_prompt[l], 0, None, None, zero_gdn_conv, zero_gdn_s, zero_lru_conv, zero_lru_h, lw)
        kp_l.append(k_new); vp_l.append(v_new); gcp_l.append(gconv); gsp_l.append(gstate)
        lcp_l.append(lconv); lhp_l.append(lh)
        past_k = cache_k[page_table[:, :, None], page_rows, l].reshape(bs, past_len, MOBA_HEADS, MOBA_HEAD_DIM)
        past_v = cache_v[page_table[:, :, None], page_rows, l].reshape(bs, past_len, MOBA_HEADS, MOBA_HEAD_DIM)
        ys, k_new, v_new, gconv, gstate, lconv, lh = trunk_layer(
            ys, p_sample[l], past_len, past_k, past_v, state_gdn_conv[:, l], state_gdn[:, l],
            state_lru_conv[:, l], state_lru_h[:, l], lw)
        ks_l.append(k_new); vs_l.append(v_new); gcs_l.append(gconv); gss_l.append(gstate)
        lcs_l.append(lconv); lhs_l.append(lh)
    k_prompt = jnp.stack(kp_l, axis=2)
    v_prompt = jnp.stack(vp_l, axis=2)
    k_sample = jnp.stack(ks_l, axis=2)
    v_sample = jnp.stack(vs_l, axis=2)
    gdn_state_prompt = jnp.stack(gsp_l, axis=1)
    gdn_state_sample = jnp.stack(gss_l, axis=1)
    gdn_conv_prompt = jnp.stack(gcp_l, axis=1)
    gdn_conv_sample = jnp.stack(gcs_l, axis=1)
    lru_h_prompt = jnp.stack(lhp_l, axis=1)
    lru_h_sample = jnp.stack(lhs_l, axis=1)
    lru_conv_prompt = jnp.stack(lcp_l, axis=1)
    lru_conv_sample = jnp.stack(lcs_l, axis=1)
    return (yp, ys, k_prompt, v_prompt, k_sample, v_sample, gdn_state_prompt, gdn_state_sample,
            gdn_conv_prompt, gdn_conv_sample, lru_h_prompt, lru_h_sample, lru_conv_prompt, lru_conv_sample)
```

```python
import functools
import math

import jax
import jax.numpy as jnp
from jax import lax
from jax.experimental import pallas as pl
from jax.experimental.pallas import tpu as pltpu

F32 = jnp.float32
BF16 = jnp.bfloat16

D_MODEL = 1024
MOBA_HEADS = 8
MOBA_HEAD_DIM = 64
MOBA_WIDTH = MOBA_HEADS * MOBA_HEAD_DIM
MOBA_BLOCK = 256
MOBA_TOPK = 3
ROPE_THETA = 10000.0
PAGE_SIZE = 128
GDN_HEADS = 4
GDN_DK = 128
GDN_DV = 128
GDN_KEY_WIDTH = GDN_HEADS * GDN_DK
GDN_WIDTH = GDN_HEADS * GDN_DV
GDN_CONV_DIM = 2 * GDN_KEY_WIDTH + GDN_WIDTH
GDN_CHUNK = 64
CONV_WIDTH = 4
LRU_WIDTH = 512
LRU_HEADS = 8
LRU_BLOCK = LRU_WIDTH // LRU_HEADS
LRU_C = 8.0
D_FF = 4 * D_MODEL
PLE_DIM = 256
EPS = 1e-6

U_GATES = 0
U_MOBA = 3 * D_MODEL
U_GDN = U_MOBA + 3 * MOBA_WIDTH
U_Z = U_GDN + GDN_CONV_DIM
U_X = U_Z + GDN_WIDTH
U_Y = U_X + LRU_WIDTH
U_COLS = U_Y + LRU_WIDTH
AB_COLS = 128

VMEM_LIMIT = 56 * 1024 * 1024
LANES = 128


def _params(*sem):
    return pltpu.CompilerParams(dimension_semantics=sem, vmem_limit_bytes=VMEM_LIMIT)


def _dot(a, b):
    return jnp.dot(a, b, preferred_element_type=F32)


def _dot_nt(a, b):
    return lax.dot_general(a, b, (((1,), (1,)), ((), ())), preferred_element_type=F32)


def _dot_tn(a, b):
    return lax.dot_general(a, b, (((0,), (0,)), ((), ())), preferred_element_type=F32)


def _split2(x):
    hi = x.astype(BF16)
    lo = (x - hi.astype(F32)).astype(BF16)
    return hi, lo


def _split3(x):
    hi = x.astype(BF16)
    r = x - hi.astype(F32)
    mid = r.astype(BF16)
    lo = (r - mid.astype(F32)).astype(BF16)
    return hi, mid, lo


def _dot_hp(a, b):
    ah, al = _split2(a)
    bh, bl = _split2(b)
    return _dot(ah, bh) + (_dot(ah, bl) + _dot(al, bh))


def _dot_lhs_exact(m_bf16, x):
    h, m, l = _split3(x)
    return _dot(m_bf16, h) + (_dot(m_bf16, m) + _dot(m_bf16, l))


def _dot_rhs_exact(x, m_bf16):
    h, m, l = _split3(x)
    return _dot(h, m_bf16) + (_dot(m, m_bf16) + _dot(l, m_bf16))


def _rms(xf, gain):
    ms = jnp.mean(xf * xf, axis=-1, keepdims=True)
    return xf * lax.rsqrt(ms + EPS) * gain


def _sigmoid(x):
    return 1.0 / (1.0 + jnp.exp(-x))


def _silu(x):
    return x * _sigmoid(x)


def _softplus(x):
    return jnp.maximum(x, 0.0) + jnp.log1p(jnp.exp(-jnp.abs(x)))


def _expm1(x):
    u = jnp.exp(x)
    um1 = u - 1.0
    lu = jnp.log(u)
    near = um1 * x / jnp.where(lu == 0.0, 1.0, lu)
    near = jnp.where(um1 == 0.0, x, near)
    return jnp.where(jnp.abs(x) < 0.5, near, um1)


def _norm_matmul_kernel(x_ref, g_ref, w_ref, o_ref, h_ref):
    @pl.when(pl.program_id(1) == 0)
    def _():
        h_ref[...] = _rms(x_ref[...], g_ref[...]).astype(BF16)

    o_ref[...] = _dot(h_ref[...], w_ref[...])


def norm_matmul(x, gain, w, layer, tm, tn):
    m, d = x.shape
    n = w.shape[-1]
    return pl.pallas_call(
        _norm_matmul_kernel,
        out_shape=jax.ShapeDtypeStruct((m, n), F32),
        grid=(m // tm, n // tn),
        in_specs=[
            pl.BlockSpec((tm, d), lambda i, j: (i, 0)),
            pl.BlockSpec((None, 1, d), lambda i, j: (layer, 0, 0)),
            pl.BlockSpec((None, d, tn), lambda i, j: (layer, 0, j)),
        ],
        out_specs=pl.BlockSpec((tm, tn), lambda i, j: (i, j)),
        scratch_shapes=[pltpu.VMEM((tm, d), BF16)],
        compiler_params=_params("parallel", "arbitrary"),
        name="norm_matmul",
    )(x, gain, w)


def _moba_prep_kernel(qkv_ref, cos_ref, sin_ref, bd_ref, gq_ref, gk_ref, q_ref, k_ref, v_ref, *hm_refs):
    cos = cos_ref[...]
    sin = sin_ref[...]
    bd = bd_ref[...]
    lane = lax.broadcasted_iota(jnp.int32, cos.shape, 1)
    first_half = (lane % MOBA_HEAD_DIM) < (MOBA_HEAD_DIM // 2)

    def norm_rot(x, gain):
        ms = _dot_rhs_exact(x * x, bd)
        y = x * lax.rsqrt(ms + EPS) * gain
        partner = jnp.where(first_half,
                            pltpu.roll(y, MOBA_WIDTH - MOBA_HEAD_DIM // 2, 1),
                            pltpu.roll(y, MOBA_HEAD_DIM // 2, 1))
        return y * cos + partner * sin

    q = norm_rot(qkv_ref[:, 0:MOBA_WIDTH], gq_ref[...])
    k = norm_rot(qkv_ref[:, MOBA_WIDTH:2 * MOBA_WIDTH], gk_ref[...])
    v = qkv_ref[:, 2 * MOBA_WIDTH:3 * MOBA_WIDTH]
    q_ref[...] = q
    k_ref[...] = k
    v_ref[...] = v
    if hm_refs:
        qh_ref, kh_ref, vh_ref = hm_refs
        for h in range(MOBA_HEADS):
            sl = slice(h * MOBA_HEAD_DIM, (h + 1) * MOBA_HEAD_DIM)
            qh_ref[0, h] = q[:, sl].astype(BF16)
            kh_ref[0, h] = k[:, sl].astype(BF16)
            vh_ref[0, h] = v[:, sl].astype(BF16)


def moba_prep(u, cos, sin, bd, gq, gk, layer, tq, seq_blocks, head_major):
    m = u.shape[0]
    n_steps = m // tq
    out_shape = [jax.ShapeDtypeStruct((m, MOBA_WIDTH), F32)] * 3
    out_specs = [pl.BlockSpec((tq, MOBA_WIDTH), lambda i: (i, 0))] * 3
    if head_major:
        nb = n_steps // seq_blocks
        hm = jax.ShapeDtypeStruct((nb, MOBA_HEADS, seq_blocks * tq, MOBA_HEAD_DIM), BF16)
        out_shape += [hm] * 3
        out_specs += [pl.BlockSpec((1, MOBA_HEADS, tq, MOBA_HEAD_DIM),
                                   lambda i: (i // seq_blocks, 0, i % seq_blocks, 0))] * 3
    tab = pl.BlockSpec((tq, MOBA_WIDTH), lambda i: (i % seq_blocks, 0))
    vec = pl.BlockSpec((None, 1, MOBA_WIDTH), lambda i: (layer, 0, 0))
    return pl.pallas_call(
        _moba_prep_kernel,
        out_shape=out_shape,
        grid=(n_steps,),
        in_specs=[
            pl.BlockSpec((tq, 3 * MOBA_WIDTH), lambda i: (i, U_MOBA // (3 * MOBA_WIDTH))),
            tab, tab,
            pl.BlockSpec((MOBA_WIDTH, MOBA_WIDTH), lambda i: (0, 0)),
            vec, vec,
        ],
        out_specs=out_specs,
        compiler_params=_params("parallel"),
        name="moba_prep",
    )(u, cos, sin, bd, gq, gk)


def _topk_select(gates):
    n = len(gates)
    if n <= MOBA_TOPK:
        return [None] * n
    sel = []
    for a in range(n):
        rank = jnp.zeros(gates[a].shape, F32)
        for b in range(n):
            if b == a:
                continue
            beats = (gates[b] >= gates[a]) if b < a else (gates[b] > gates[a])
            rank = rank + jnp.where(beats, 1.0, 0.0)
        sel.append(rank < float(MOBA_TOPK))
    return sel


def _moba_prompt_kernel(q_ref, k_ref, v_ref, o_ref, *, n_blk):
    blk = MOBA_BLOCK
    scale = MOBA_HEAD_DIM ** -0.5
    row = lax.broadcasted_iota(jnp.int32, (blk, blk), 0)
    col = lax.broadcasted_iota(jnp.int32, (blk, blk), 1)
    causal = row >= col
    for i in range(n_blk):
        q = q_ref[0, 0, i * blk:(i + 1) * blk, :]
        k = k_ref[0, 0, 0:(i + 1) * blk, :]
        v = v_ref[0, 0, 0:(i + 1) * blk, :]
        s = _dot_nt(q, k)
        parts = [s[:, j * blk:(j + 1) * blk] for j in range(i + 1)]
        gates = [jnp.sum(parts[j], axis=-1, keepdims=True) for j in range(i)]
        sel = _topk_select(gates)
        masked = []
        for j in range(i):
            masked.append(parts[j] if sel[j] is None else jnp.where(sel[j], parts[j], -jnp.inf))
        masked.append(jnp.where(causal, parts[i], -jnp.inf))
        mx = masked[0].max(axis=-1, keepdims=True)
        for j in range(1, i + 1):
            mx = jnp.maximum(mx, masked[j].max(axis=-1, keepdims=True))
        den = jnp.zeros((blk, 1), F32)
        acc = jnp.zeros((blk, MOBA_HEAD_DIM), F32)
        for j in range(i + 1):
            p = jnp.exp((masked[j] - mx) * scale)
            den = den + jnp.sum(p, axis=-1, keepdims=True)
            acc = acc + _dot(p.astype(BF16), v[j * blk:(j + 1) * blk, :])
        o_ref[0, 0, i * blk:(i + 1) * blk, :] = (acc / den).astype(BF16)


def moba_prompt(qh, kh, vh):
    nb, nh, t, dh = qh.shape
    spec = pl.BlockSpec((1, 1, t, dh), lambda b, h: (b, h, 0, 0))
    return pl.pallas_call(
        functools.partial(_moba_prompt_kernel, n_blk=t // MOBA_BLOCK),
        out_shape=jax.ShapeDtypeStruct((nb, nh, t, dh), BF16),
        grid=(nb, nh),
        in_specs=[spec, spec, spec],
        out_specs=spec,
        compiler_params=_params("parallel", "parallel"),
        name="moba_prompt",
    )(qh, kh, vh)


def _moba_sample_kernel(pt_ref, q_ref, kn_ref, vn_ref, kc_ref, vc_ref, o_ref, s_all, v_all, *, n_pages, t_new):
    del pt_ref
    p = pl.program_id(1)
    rows = t_new * MOBA_HEADS
    scale = MOBA_HEAD_DIM ** -0.5
    q = q_ref[0]
    head_of_row = lax.broadcasted_iota(jnp.int32, (MOBA_HEADS, MOBA_WIDTH), 0)
    head_of_lane = lax.broadcasted_iota(jnp.int32, (MOBA_HEADS, MOBA_WIDTH), 1) // MOBA_HEAD_DIM
    own_head = head_of_row == head_of_lane
    q_bd = jnp.concatenate(
        [jnp.where(own_head, jnp.broadcast_to(q[t:t + 1, :], (MOBA_HEADS, MOBA_WIDTH)), 0.0) for t in range(t_new)],
        axis=0)
    s_all[p] = _dot_nt(q_bd.astype(BF16), kc_ref[0].astype(BF16))
    v_all[p] = vc_ref[0].astype(BF16)

    @pl.when(p == n_pages - 1)
    def _():
        pages_per_blk = MOBA_BLOCK // PAGE_SIZE
        n_past = n_pages // pages_per_blk
        pages = [s_all[j] for j in range(n_pages)]
        gates = []
        for n in range(n_past):
            g = jnp.sum(pages[n * pages_per_blk], axis=-1, keepdims=True)
            for r in range(1, pages_per_blk):
                g = g + jnp.sum(pages[n * pages_per_blk + r], axis=-1, keepdims=True)
            gates.append(g)
        sel = _topk_select(gates)
        masked = []
        for j in range(n_pages):
            s_n = sel[j // pages_per_blk]
            masked.append(pages[j] if s_n is None else jnp.where(s_n, pages[j], -jnp.inf))
        tok_of_row = lax.broadcasted_iota(jnp.int32, (rows, 1), 0) // MOBA_HEADS
        kn = kn_ref[0]
        vn = vn_ref[0]
        own = []
        for j in range(t_new):
            s_j = jnp.sum(q_bd * kn[j:j + 1, :], axis=-1, keepdims=True)
            own.append(jnp.where(tok_of_row >= j, s_j, -jnp.inf))
        mx = own[0]
        for j in range(1, t_new):
            mx = jnp.maximum(mx, own[j])
        for j in range(n_pages):
            mx = jnp.maximum(mx, masked[j].max(axis=-1, keepdims=True))
        den = jnp.zeros((rows, 1), F32)
        acc = jnp.zeros((rows, MOBA_WIDTH), F32)
        for j in range(t_new):
            pj = jnp.exp((own[j] - mx) * scale)
            den = den + pj
            acc = acc + pj * vn[j:j + 1, :]
        for j in range(n_pages):
            pj = jnp.exp((masked[j] - mx) * scale)
            den = den + jnp.sum(pj, axis=-1, keepdims=True)
            acc = acc + _dot(pj.astype(BF16), v_all[j])
        acc = acc / den
        outs = []
        for t in range(t_new):
            a_t = acc[t * MOBA_HEADS:(t + 1) * MOBA_HEADS, :]
            outs.append(jnp.sum(jnp.where(own_head, a_t, 0.0), axis=0, keepdims=True))
        o_ref[0] = jnp.concatenate(outs, axis=0)


def moba_sample(page_table, q, k_new, v_new, cache_k, cache_v, layer):
    nb, t_new, _ = q.shape
    n_pages = page_table.shape[1]
    tok = pl.BlockSpec((1, t_new, MOBA_WIDTH), lambda b, p, pt: (b, 0, 0))
    page = pl.BlockSpec((1, PAGE_SIZE, MOBA_WIDTH), lambda b, p, pt: (pt[b * n_pages + p], 0, layer))
    rows = t_new * MOBA_HEADS
    return pl.pallas_call(
        functools.partial(_moba_sample_kernel, n_pages=n_pages, t_new=t_new),
        out_shape=jax.ShapeDtypeStruct((nb, t_new, MOBA_WIDTH), F32),
        grid_spec=pltpu.PrefetchScalarGridSpec(
            num_scalar_prefetch=1,
            grid=(nb, n_pages),
            in_specs=[tok, tok, tok, page, page],
            out_specs=tok,
            scratch_shapes=[pltpu.VMEM((n_pages, rows, PAGE_SIZE), F32),
                            pltpu.VMEM((n_pages, PAGE_SIZE, MOBA_WIDTH), BF16)],
        ),
        compiler_params=_params("parallel", "arbitrary"),
        name="moba_sample",
    )(page_table.reshape(-1), q, k_new, v_new, cache_k, cache_v)


def _unit_lower_inverse(low, c):
    row = lax.broadcasted_iota(jnp.int32, (c, c), 0)
    col = lax.broadcasted_iota(jnp.int32, (c, c), 1)
    x = jnp.where(row == col, 1.0, 0.0) - jnp.where((row // 2 == col // 2), low, 0.0)
    s = 2
    while s < c:
        e = jnp.where((row // (2 * s) == col // (2 * s)) & (row // s != col // s), low, 0.0)
        x = x - _dot_hp(_dot_hp(x, e), x)
        s *= 2
    return x


def _gdn_kernel(qkv_ref, z_ref, ab_ref, cw_ref, alog_ref, dtb_ref, gn_ref, cbuf_ref, s0_ref,
                o_ref, cnew_ref, snew_ref, xx_ref, st_ref, *, c, t_valid):
    ci = pl.program_id(1)

    @pl.when(ci == 0)
    def _():
        xx_ref[0:8, :] = cbuf_ref[0]
        st_ref[...] = s0_ref[0]

    xx_ref[8:8 + c, :] = qkv_ref[0]
    w = cw_ref[...]
    y = (w[0:1, :] * xx_ref[5:5 + c, :] + w[1:2, :] * xx_ref[6:6 + c, :]
         + w[2:3, :] * xx_ref[7:7 + c, :] + w[3:4, :] * xx_ref[8:8 + c, :])
    cnew_ref[0] = xx_ref[5 + t_valid:8 + t_valid, :]
    xx_ref[0:8, :] = xx_ref[c:c + 8, :]
    y = _silu(y)

    masked = t_valid < c
    valid = lax.broadcasted_iota(jnp.int32, (c, 1), 0) < t_valid
    row = lax.broadcasted_iota(jnp.int32, (c, c), 0)
    col = lax.broadcasted_iota(jnp.int32, (c, c), 1)
    tri = row >= col
    eye = row == col

    ab = ab_ref[0]
    g_all = -jnp.exp(alog_ref[...]) * _softplus(ab + dtb_ref[...])
    beta_all = _sigmoid(ab)
    if masked:
        g_all = jnp.where(valid, g_all, 0.0)
    gc_all = _dot_lhs_exact(jnp.where(tri, 1.0, 0.0).astype(BF16), g_all)

    gn = gn_ref[...]
    for h in range(GDN_HEADS):
        lo, hi = h * GDN_DK, (h + 1) * GDN_DK
        q = y[:, lo:hi]
        k = y[:, GDN_KEY_WIDTH + lo:GDN_KEY_WIDTH + hi]
        v = y[:, 2 * GDN_KEY_WIDTH + lo:2 * GDN_KEY_WIDTH + hi]
        q = q * lax.rsqrt(jnp.sum(q * q, axis=-1, keepdims=True) + EPS) * (GDN_DK ** -0.5)
        k = k * lax.rsqrt(jnp.sum(k * k, axis=-1, keepdims=True) + EPS)
        beta = beta_all[:, GDN_HEADS + h:GDN_HEADS + h + 1]
        if masked:
            k = jnp.where(valid, k, 0.0)
            v = jnp.where(valid, v, 0.0)
            beta = jnp.where(valid, beta, 0.0)
        g_col = gc_all[:, h:h + 1]
        g_row = jnp.sum(jnp.where(eye, g_col, 0.0), axis=0, keepdims=True)
        g_last = gc_all[c - 1:c, h:h + 1]
        decay = jnp.exp(jnp.where(tri, g_col - g_row, -jnp.inf))
        kb = k * beta
        kb16 = kb.astype(BF16)
        k16 = k.astype(BF16)
        low = jnp.where(row > col, _dot_nt(kb16, k16) * decay, 0.0)
        t_mat = _unit_lower_inverse(low, c).astype(BF16)
        eg = jnp.exp(g_col)
        u = _dot(t_mat, (v * beta).astype(BF16))
        wk = _dot(t_mat, (kb * eg).astype(BF16))
        a_intra = _dot_nt(q.astype(BF16), k16) * decay
        q_dec = q * eg
        k_dec = k * jnp.exp(g_last - g_col)
        st = st_ref[h]
        st16 = st.astype(BF16)
        v_new = u - _dot(wk.astype(BF16), st16)
        v_new16 = v_new.astype(BF16)
        o = _dot(q_dec.astype(BF16), st16) + _dot(a_intra.astype(BF16), v_new16)
        st_new = st * jnp.exp(g_last) + _dot_tn(k_dec.astype(BF16), v_new16)
        st_ref[h] = st_new
        snew_ref[0, h] = st_new
        on = _rms(o, gn) * _silu(z_ref[0, :, lo:hi])
        o_ref[0, :, lo:hi] = on.astype(BF16)


def gdn(u3, ab3, conv_w, a_log, dt_bias, out_norm, conv_buf, s0, layer, c, t_valid):
    nb, t, _ = u3.shape
    vec = lambda n: pl.BlockSpec((None, 1, n), lambda b, i: (layer, 0, 0))
    return pl.pallas_call(
        functools.partial(_gdn_kernel, c=c, t_valid=t_valid),
        out_shape=[jax.ShapeDtypeStruct((nb, t, GDN_WIDTH), BF16),
                   jax.ShapeDtypeStruct((nb, CONV_WIDTH - 1, GDN_CONV_DIM), F32),
                   jax.ShapeDtypeStruct((nb, GDN_HEADS, GDN_DK, GDN_DV), F32)],
        grid=(nb, t // c),
        in_specs=[
            pl.BlockSpec((1, c, GDN_CONV_DIM), lambda b, i: (b, i, U_GDN // GDN_CONV_DIM)),
            pl.BlockSpec((1, c, GDN_WIDTH), lambda b, i: (b, i, U_Z // GDN_WIDTH)),
            pl.BlockSpec((1, c, AB_COLS), lambda b, i: (b, i, 0)),
            pl.BlockSpec((None, CONV_WIDTH, GDN_CONV_DIM), lambda b, i: (layer, 0, 0)),
            vec(AB_COLS), vec(AB_COLS), vec(GDN_DV),
            pl.BlockSpec((1, 8, GDN_CONV_DIM), lambda b, i: (b, 0, 0)),
            pl.BlockSpec((1, GDN_HEADS, GDN_DK, GDN_DV), lambda b, i: (b, 0, 0, 0)),
        ],
        out_specs=[
            pl.BlockSpec((1, c, GDN_WIDTH), lambda b, i: (b, i, 0)),
            pl.BlockSpec((1, CONV_WIDTH - 1, GDN_CONV_DIM), lambda b, i: (b, 0, 0)),
            pl.BlockSpec((1, GDN_HEADS, GDN_DK, GDN_DV), lambda b, i: (b, 0, 0, 0)),
        ],
        scratch_shapes=[pltpu.VMEM((c + 8, GDN_CONV_DIM), F32),
                        pltpu.VMEM((GDN_HEADS, GDN_DK, GDN_DV), F32)],
        compiler_params=_params("parallel", "arbitrary"),
        name="gdn",
    )(u3, u3, ab3, conv_w, a_log, dt_bias, out_norm, conv_buf, s0)


def _lru_gates(xf, wa, ba, wx, bx, lam):
    x16 = xf.astype(BF16)
    r = _sigmoid(_dot(x16, wa) + ba)
    i = _sigmoid(_dot(x16, wx) + bx)
    log_a = -LRU_C * r * _softplus(-lam)
    a = jnp.exp(log_a)
    b = jnp.sqrt(-_expm1(2.0 * log_a)) * (i * xf)
    return a, b


def _lru_prompt_kernel(x_ref, y_ref, cw_ref, cb_ref, wa_ref, ba_ref, wx_ref, bx_ref, lam_ref, cbuf_ref, h0_ref,
                       o_ref, cnew_ref, hlast_ref, xx_ref, a_s, b_s, h_s, hcar, *, nb, tc):
    ci = pl.program_id(0)

    @pl.when(ci == 0)
    def _():
        xx_ref[:, 0:8, :] = cbuf_ref[...]
        hcar[...] = h0_ref[...]

    xx_ref[:, 8:8 + tc, :] = x_ref[...]
    w = cw_ref[...]
    xc = (w[0:1, :] * xx_ref[:, 5:5 + tc, :] + w[1:2, :] * xx_ref[:, 6:6 + tc, :]
          + w[2:3, :] * xx_ref[:, 7:7 + tc, :] + w[3:4, :] * xx_ref[:, 8:8 + tc, :]) + cb_ref[...]
    cnew_ref[...] = xx_ref[:, tc + 5:tc + 8, :]
    xx_ref[:, 0:8, :] = xx_ref[:, tc:tc + 8, :]
    xf = xc.reshape(nb * tc, LRU_WIDTH)
    a, b = _lru_gates(xf, wa_ref[...], ba_ref[...], wx_ref[...], bx_ref[...], lam_ref[...])
    n_lane_tiles = LRU_WIDTH // LANES
    for j in range(n_lane_tiles):
        a_s[j] = a[:, j * LANES:(j + 1) * LANES]
        b_s[j] = b[:, j * LANES:(j + 1) * LANES]

    def step(t, hs):
        new = []
        for j in range(n_lane_tiles):
            h = a_s[j, pl.ds(t, nb, stride=tc), :] * hs[j] + b_s[j, pl.ds(t, nb, stride=tc), :]
            h_s[j, pl.ds(t, nb, stride=tc), :] = h
            new.append(h)
        return tuple(new)

    h0 = hcar[...]
    hs = lax.fori_loop(0, tc, step, tuple(h0[:, j * LANES:(j + 1) * LANES] for j in range(n_lane_tiles)))
    h = jnp.concatenate(hs, axis=1)
    hcar[...] = h
    hlast_ref[...] = h
    h_all = jnp.concatenate([h_s[j] for j in range(n_lane_tiles)], axis=1)
    out = h_all * jax.nn.gelu(y_ref[...].reshape(nb * tc, LRU_WIDTH))
    o_ref[...] = out.reshape(nb, tc, LRU_WIDTH).astype(BF16)


def lru_prompt(u3, conv_w, conv_b, wa, ba, wx, bx, lam, conv_buf, h0, layer, tc):
    nb, t, _ = u3.shape
    vec = pl.BlockSpec((None, 1, LRU_WIDTH), lambda i: (layer, 0, 0))
    mat = pl.BlockSpec((None, LRU_WIDTH, LRU_WIDTH), lambda i: (layer, 0, 0))
    return pl.pallas_call(
        functools.partial(_lru_prompt_kernel, nb=nb, tc=tc),
        out_shape=[jax.ShapeDtypeStruct((nb, t, LRU_WIDTH), BF16),
                   jax.ShapeDtypeStruct((nb, CONV_WIDTH - 1, LRU_WIDTH), F32),
                   jax.ShapeDtypeStruct((nb, LRU_WIDTH), F32)],
        grid=(t // tc,),
        in_specs=[
            pl.BlockSpec((nb, tc, LRU_WIDTH), lambda i: (0, i, U_X // LRU_WIDTH)),
            pl.BlockSpec((nb, tc, LRU_WIDTH), lambda i: (0, i, U_Y // LRU_WIDTH)),
            pl.BlockSpec((None, CONV_WIDTH, LRU_WIDTH), lambda i: (layer, 0, 0)),
            vec, mat, vec, mat, vec, vec,
            pl.BlockSpec((nb, 8, LRU_WIDTH), lambda i: (0, 0, 0)),
            pl.BlockSpec((nb, LRU_WIDTH), lambda i: (0, 0)),
        ],
        out_specs=[
            pl.BlockSpec((nb, tc, LRU_WIDTH), lambda i: (0, i, 0)),
            pl.BlockSpec((nb, CONV_WIDTH - 1, LRU_WIDTH), lambda i: (0, 0, 0)),
            pl.BlockSpec((nb, LRU_WIDTH), lambda i: (0, 0)),
        ],
        scratch_shapes=[pltpu.VMEM((nb, tc + 8, LRU_WIDTH), F32),
                        pltpu.VMEM((LRU_WIDTH // LANES, nb * tc, LANES), F32),
                        pltpu.VMEM((LRU_WIDTH // LANES, nb * tc, LANES), F32),
                        pltpu.VMEM((LRU_WIDTH // LANES, nb * tc, LANES), F32),
                        pltpu.VMEM((nb, LRU_WIDTH), F32)],
        compiler_params=_params("arbitrary"),
        name="lru_prompt",
    )(u3, u3, conv_w, conv_b, wa, ba, wx, bx, lam, conv_buf, h0)


def _lru_sample_kernel(x_ref, y_ref, cw_ref, cb_ref, wa_ref, ba_ref, wx_ref, bx_ref, lam_ref, cbuf_ref, h0_ref,
                       o_ref, cnew_ref, hlast_ref, *, t_new):
    w = cw_ref[...]
    xx = [cbuf_ref[j] for j in range(CONV_WIDTH - 1)] + [x_ref[j] for j in range(t_new)]
    for j in range(CONV_WIDTH - 1):
        cnew_ref[j] = xx[t_new + j]
    h = h0_ref[...]
    for t in range(t_new):
        xf = cb_ref[...] + w[0:1, :] * xx[t]
        for j in range(1, CONV_WIDTH):
            xf = xf + w[j:j + 1, :] * xx[t + j]
        a, b = _lru_gates(xf, wa_ref[...], ba_ref[...], wx_ref[...], bx_ref[...], lam_ref[...])
        h = a * h + b
        o_ref[t] = (h * jax.nn.gelu(y_ref[t])).astype(BF16)
    hlast_ref[...] = h


def lru_sample(x_tm, y_tm, conv_w, conv_b, wa, ba, wx, bx, lam, conv_buf_tm, h0, layer):
    t_new, nb, _ = x_tm.shape
    vec = pl.BlockSpec((None, 1, LRU_WIDTH), lambda i: (layer, 0, 0))
    mat = pl.BlockSpec((None, LRU_WIDTH, LRU_WIDTH), lambda i: (layer, 0, 0))
    tok = pl.BlockSpec((t_new, nb, LRU_WIDTH), lambda i: (0, 0, 0))
    buf = pl.BlockSpec((CONV_WIDTH - 1, nb, LRU_WIDTH), lambda i: (0, 0, 0))
    st = pl.BlockSpec((nb, LRU_WIDTH), lambda i: (0, 0))
    return pl.pallas_call(
        functools.partial(_lru_sample_kernel, t_new=t_new),
        out_shape=[jax.ShapeDtypeStruct((t_new, nb, LRU_WIDTH), BF16),
                   jax.ShapeDtypeStruct((CONV_WIDTH - 1, nb, LRU_WIDTH), F32),
                   jax.ShapeDtypeStruct((nb, LRU_WIDTH), F32)],
        grid=(1,),
        in_specs=[tok, tok, pl.BlockSpec((None, CONV_WIDTH, LRU_WIDTH), lambda i: (layer, 0, 0)),
                  vec, mat, vec, mat, vec, vec, buf, st],
        out_specs=[tok, buf, st],
        compiler_params=_params("arbitrary"),
        name="lru_sample",
    )(x_tm, y_tm, conv_w, conv_b, wa, ba, wx, bx, lam, conv_buf_tm, h0)


def _merge_kernel(oa_ref, od_ref, or_ref, ga_ref, gd_ref, gr_ref, x_ref, wa_ref, wd_ref, wr_ref, wo_ref, o_ref):
    merged = (_sigmoid(ga_ref[...]) * _dot(oa_ref[...], wa_ref[...])
              + _sigmoid(gd_ref[...]) * _dot(od_ref[...], wd_ref[...])
              + _sigmoid(gr_ref[...]) * _dot(or_ref[...], wr_ref[...]))
    o_ref[...] = x_ref[...] + _dot(merged.astype(BF16), wo_ref[...])


def merge(o_a, o_d, o_r, u, x, w_a, w_d, w_r, w_out, layer, tm):
    m, d = x.shape
    br = lambda n: pl.BlockSpec((tm, n), lambda i: (i, 0))
    gate = lambda j: pl.BlockSpec((tm, d), lambda i: (i, j))
    wspec = lambda k: pl.BlockSpec((None, k, d), lambda i: (layer, 0, 0))
    return pl.pallas_call(
        _merge_kernel,
        out_shape=jax.ShapeDtypeStruct((m, d), F32),
        grid=(m // tm,),
        in_specs=[br(MOBA_WIDTH), br(GDN_WIDTH), br(LRU_WIDTH), gate(0), gate(1), gate(2), br(d),
                  wspec(MOBA_WIDTH), wspec(GDN_WIDTH), wspec(LRU_WIDTH), wspec(d)],
        out_specs=br(d),
        compiler_params=_params("parallel"),
        name="merge",
    )(o_a, o_d, o_r, u, u, u, x, w_a, w_d, w_r, w_out)


def _ffn_kernel(x_ref, g_ref, wu_ref, wd_ref, o_ref, h_ref, acc_ref):
    f = pl.program_id(1)

    @pl.when(f == 0)
    def _():
        h_ref[...] = _rms(x_ref[...], g_ref[...]).astype(BF16)
        acc_ref[...] = jnp.zeros_like(acc_ref)

    a = jnp.maximum(_dot(h_ref[...], wu_ref[...]), 0.0)
    acc_ref[...] += _dot((a * a).astype(BF16), wd_ref[...])

    @pl.when(f == pl.num_programs(1) - 1)
    def _():
        o_ref[...] = x_ref[...] + acc_ref[...]


def ffn(x, gain, w_up, w_down, layer, tm, tf):
    m, d = x.shape
    f = w_up.shape[-1]
    return pl.pallas_call(
        _ffn_kernel,
        out_shape=jax.ShapeDtypeStruct((m, d), F32),
        grid=(m // tm, f // tf),
        in_specs=[
            pl.BlockSpec((tm, d), lambda i, j: (i, 0)),
            pl.BlockSpec((None, 1, d), lambda i, j: (layer, 0, 0)),
            pl.BlockSpec((None, d, tf), lambda i, j: (layer, 0, j)),
            pl.BlockSpec((None, tf, d), lambda i, j: (layer, j, 0)),
        ],
        out_specs=pl.BlockSpec((tm, d), lambda i, j: (i, 0)),
        scratch_shapes=[pltpu.VMEM((tm, d), BF16), pltpu.VMEM((tm, d), F32)],
        compiler_params=_params("parallel", "arbitrary"),
        name="ffn",
    )(x, gain, w_up, w_down)


def _ple_kernel(x_ref, g_ref, wg_ref, p_ref, wp_ref, o_ref):
    x = x_ref[...]
    gate = _sigmoid(_dot(_rms(x, g_ref[...]).astype(BF16), wg_ref[...]))
    o_ref[...] = x + gate * _dot(p_ref[...].astype(BF16), wp_ref[...])


def ple(x, gain, w_gate, p, w_proj, layer, tm):
    m, d = x.shape
    steps = m // tm
    return pl.pallas_call(
        _ple_kernel,
        out_shape=jax.ShapeDtypeStruct((m, d), F32),
        grid=(steps,),
        in_specs=[
            pl.BlockSpec((tm, d), lambda i: (i, 0)),
            pl.BlockSpec((None, 1, d), lambda i: (layer, 0, 0)),
            pl.BlockSpec((None, d, d), lambda i: (layer, 0, 0)),
            pl.BlockSpec((tm, PLE_DIM), lambda i: (layer * steps + i, 0)),
            pl.BlockSpec((None, PLE_DIM, d), lambda i: (layer, 0, 0)),
        ],
        out_specs=pl.BlockSpec((tm, d), lambda i: (i, 0)),
        compiler_params=_params("parallel"),
        name="ple",
    )(x, gain, w_gate, p, w_proj)


def _rope_tables(pos):
    half = MOBA_HEAD_DIM // 2
    inv_freq = ROPE_THETA ** (-jnp.arange(half, dtype=F32) / half)
    ang = pos.astype(F32)[:, None] * inv_freq[None, :]
    cos = jnp.cos(ang)
    sin = jnp.sin(ang)
    cos_h = jnp.concatenate([cos, cos], axis=-1)
    sin_h = jnp.concatenate([-sin, sin], axis=-1)
    return jnp.tile(cos_h, (1, MOBA_HEADS)), jnp.tile(sin_h, (1, MOBA_HEADS))


def _block_diag(w):
    l, h, n, _ = w.shape
    eye = jnp.eye(h, dtype=w.dtype)
    return (w[:, :, :, None, :] * eye[None, :, None, :, None]).reshape(l, h * n, h * n)


def _row3(v):
    return v[:, None, :]


def _pad_lanes(v, n):
    return jnp.pad(v, ((0, 0), (0, n - v.shape[-1])))[:, None, :]


def kernel(x_prompt, x_sample, cache_k, cache_v, state_gdn, state_gdn_conv, state_lru_h, state_lru_conv,
           page_table, p_prompt, p_sample, g_mix, w_in, moba_q_norm, moba_k_norm, w_branch_a, gdn_conv_w,
           gdn_a_log, gdn_dt_bias, gdn_out_norm, w_branch_d, lru_conv_w, lru_conv_b, lru_wa, lru_ba, lru_wx,
           lru_bx, lru_lambda, w_branch_r, w_out, g_ffn, w_up, w_down, g_ple, w_ple_gate, w_ple_proj):
    depth = w_in.shape[0]
    bp, seq, d = x_prompt.shape
    bs, t_new, _ = x_sample.shape
    n_pages = page_table.shape[1]
    past_len = n_pages * PAGE_SIZE
    mp, ms = bp * seq, bs * t_new

    o = 0
    offs = []
    for size in (3 * MOBA_WIDTH, GDN_CONV_DIM, GDN_WIDTH, GDN_HEADS, GDN_HEADS, LRU_WIDTH, LRU_WIDTH, 3 * D_MODEL):
        offs.append((o, o + size))
        o += size
    (m0, m1), (d0, d1), (z0, z1), (a0, a1), (b0, b1), (x0, x1), (y0, y1), (g0, g1) = offs
    w_main = jnp.concatenate([w_in[:, :, g0:g1], w_in[:, :, m0:m1], w_in[:, :, d0:d1], w_in[:, :, z0:z1],
                              w_in[:, :, x0:x1], w_in[:, :, y0:y1]], axis=-1).astype(BF16)
    w_ab = jnp.pad(w_in[:, :, a0:b1], ((0, 0), (0, 0), (0, AB_COLS - 2 * GDN_HEADS))).astype(BF16)
    w_a16, w_d16, w_r16, w_o16 = (w.astype(BF16) for w in (w_branch_a, w_branch_d, w_branch_r, w_out))
    w_up16, w_down16, w_pg16, w_pp16 = (w.astype(BF16) for w in (w_up, w_down, w_ple_gate, w_ple_proj))
    wa_bd = _block_diag(lru_wa).astype(BF16)
    wx_bd = _block_diag(lru_wx).astype(BF16)
    head_mean = jnp.kron(jnp.eye(MOBA_HEADS, dtype=F32),
                         jnp.full((MOBA_HEAD_DIM, MOBA_HEAD_DIM), 1.0 / MOBA_HEAD_DIM, F32)).astype(BF16)
    gq = _row3(jnp.tile(moba_q_norm, (1, MOBA_HEADS)))
    gk = _row3(jnp.tile(moba_k_norm, (1, MOBA_HEADS)))
    g_mix3, g_ffn3, g_ple3 = _row3(g_mix), _row3(g_ffn), _row3(g_ple)
    a_log3 = _pad_lanes(gdn_a_log, AB_COLS)
    dt_bias3 = _pad_lanes(gdn_dt_bias, AB_COLS)
    out_norm3 = _row3(gdn_out_norm)
    lru_cb3, lru_ba3, lru_bx3, lru_lam3 = _row3(lru_conv_b), _row3(lru_ba), _row3(lru_bx), _row3(lru_lambda)

    cos_p, sin_p = _rope_tables(jnp.arange(seq, dtype=jnp.int32))
    cos_s, sin_s = _rope_tables(past_len + jnp.arange(t_new, dtype=jnp.int32))
    cos_s, sin_s = jnp.tile(cos_s, (bs, 1)), jnp.tile(sin_s, (bs, 1))

    cache_k3 = cache_k.reshape(cache_k.shape[0], PAGE_SIZE, depth * MOBA_WIDTH)
    cache_v3 = cache_v.reshape(cache_v.shape[0], PAGE_SIZE, depth * MOBA_WIDTH)
    pp = p_prompt.reshape(depth * mp, PLE_DIM)
    ps = p_sample.reshape(depth * ms, PLE_DIM)

    zero_gconv = jnp.zeros((bp, 8, GDN_CONV_DIM), F32)
    zero_gstate = jnp.zeros((bp, GDN_HEADS, GDN_DK, GDN_DV), F32)
    zero_lconv = jnp.zeros((bp, 8, LRU_WIDTH), F32)
    zero_lh = jnp.zeros((bp, LRU_WIDTH), F32)
    c_s = 16

    xp = x_prompt.reshape(mp, d)
    xs = x_sample.reshape(ms, d)
    outs = {k: [] for k in ("kp", "vp", "ks", "vs", "gsp", "gss", "gcp", "gcs", "lhp", "lhs", "lcp", "lcs")}

    for l in range(depth):
        u = norm_matmul(xp, g_mix3, w_main, l, 512, 1536)
        ab = norm_matmul(xp, g_mix3, w_ab, l, 512, AB_COLS)
        _, k_a, v_a, qh, kh, vh = moba_prep(u, cos_p, sin_p, head_mean, gq, gk, l, MOBA_BLOCK,
                                            seq // MOBA_BLOCK, True)
        o_a = moba_prompt(qh, kh, vh)
        o_a = o_a.transpose(0, 2, 1, 3).reshape(mp, MOBA_WIDTH)
        u3 = u.reshape(bp, seq, U_COLS)
        o_d, gconv, gstate = gdn(u3, ab.reshape(bp, seq, AB_COLS), gdn_conv_w, a_log3, dt_bias3, out_norm3,
                                 zero_gconv, zero_gstate, l, GDN_CHUNK, GDN_CHUNK)
        o_r, lconv, lh = lru_prompt(u3, lru_conv_w, lru_cb3, wa_bd, lru_ba3, wx_bd, lru_bx3, lru_lam3,
                                    zero_lconv, zero_lh, l, 256)
        xp = merge(o_a, o_d.reshape(mp, GDN_WIDTH), o_r.reshape(mp, LRU_WIDTH), u, xp,
                   w_a16, w_d16, w_r16, w_o16, l, 256)
        xp = ffn(xp, g_ffn3, w_up16, w_down16, l, 512, 1024)
        xp = ple(xp, g_ple3, w_pg16, pp, w_pp16, l, 512)
        outs["kp"].append(k_a); outs["vp"].append(v_a); outs["gcp"].append(gconv); outs["gsp"].append(gstate)
        outs["lcp"].append(lconv); outs["lhp"].append(lh)

        u = norm_matmul(xs, g_mix3, w_main, l, ms, 1536)
        ab = norm_matmul(xs, g_mix3, w_ab, l, ms, AB_COLS)
        q_s, k_s, v_s = moba_prep(u, cos_s, sin_s, head_mean, gq, gk, l, ms, 1, False)
        o_a = moba_sample(page_table, q_s.reshape(bs, t_new, MOBA_WIDTH), k_s.reshape(bs, t_new, MOBA_WIDTH),
                          v_s.reshape(bs, t_new, MOBA_WIDTH), cache_k3, cache_v3, l)
        o_a = o_a.reshape(ms, MOBA_WIDTH).astype(BF16)
        pad_t = ((0, 0), (0, c_s - t_new), (0, 0))
        u3 = jnp.pad(u.reshape(bs, t_new, U_COLS), pad_t)
        ab3 = jnp.pad(ab.reshape(bs, t_new, AB_COLS), pad_t)
        gbuf = jnp.pad(state_gdn_conv[:, l], ((0, 0), (8 - (CONV_WIDTH - 1), 0), (0, 0)))
        o_d, gconv, gstate = gdn(u3, ab3, gdn_conv_w, a_log3, dt_bias3, out_norm3,
                                 gbuf, state_gdn[:, l], l, c_s, t_new)
        o_d = o_d[:, :t_new].reshape(ms, GDN_WIDTH)
        us = u.reshape(bs, t_new, U_COLS)
        x_tm = us[:, :, U_X:U_X + LRU_WIDTH].transpose(1, 0, 2)
        y_tm = us[:, :, U_Y:U_Y + LRU_WIDTH].transpose(1, 0, 2)
        o_r, lconv, lh = lru_sample(x_tm, y_tm, lru_conv_w, lru_cb3, wa_bd, lru_ba3, wx_bd, lru_bx3, lru_lam3,
                                    state_lru_conv[:, l].transpose(1, 0, 2), state_lru_h[:, l], l)
        o_r = o_r.transpose(1, 0, 2).reshape(ms, LRU_WIDTH)
        xs = merge(o_a, o_d, o_r, u, xs, w_a16, w_d16, w_r16, w_o16, l, 256)
        xs = ffn(xs, g_ffn3, w_up16, w_down16, l, ms, 1024)
        xs = ple(xs, g_ple3, w_pg16, ps, w_pp16, l, ms)
        outs["ks"].append(k_s); outs["vs"].append(v_s); outs["gcs"].append(gconv); outs["gss"].append(gstate)
        outs["lcs"].append(lconv.transpose(1, 0, 2)); outs["lhs"].append(lh)

    hd = (MOBA_HEADS, MOBA_HEAD_DIM)
    k_prompt = jnp.stack(outs["kp"], axis=1).reshape(bp, seq, depth, *hd)
    v_prompt = jnp.stack(outs["vp"], axis=1).reshape(bp, seq, depth, *hd)
    k_sample = jnp.stack(outs["ks"], axis=1).reshape(bs, t_new, depth, *hd)
    v_sample = jnp.stack(outs["vs"], axis=1).reshape(bs, t_new, depth, *hd)
    return (xp.reshape(bp, seq, d), xs.reshape(bs, t_new, d), k_prompt, v_prompt, k_sample, v_sample,
            jnp.stack(outs["gsp"], axis=1), jnp.stack(outs["gss"], axis=1),
            jnp.stack(outs["gcp"], axis=1), jnp.stack(outs["gcs"], axis=1),
            jnp.stack(outs["lhp"], axis=1), jnp.stack(outs["lhs"], axis=1),
            jnp.stack(outs["lcp"], axis=1), jnp.stack(outs["lcs"], axis=1))
```

```python
import functools
import math

import jax
import jax.numpy as jnp
from jax import lax
from jax.experimental import pallas as pl
from jax.experimental.pallas import tpu as pltpu

F32 = jnp.float32
BF16 = jnp.bfloat16

D_MODEL = 1024
MOBA_HEADS = 8
MOBA_HEAD_DIM = 64
MOBA_WIDTH = MOBA_HEADS * MOBA_HEAD_DIM
MOBA_BLOCK = 256
MOBA_TOPK = 3
ROPE_THETA = 10000.0
PAGE_SIZE = 128
GDN_HEADS = 4
GDN_DK = 128
GDN_DV = 128
GDN_KEY_WIDTH = GDN_HEADS * GDN_DK
GDN_WIDTH = GDN_HEADS * GDN_DV
GDN_CONV_DIM = 2 * GDN_KEY_WIDTH + GDN_WIDTH
GDN_CHUNK = 64
CONV_WIDTH = 4
LRU_WIDTH = 512
LRU_HEADS = 8
LRU_BLOCK = LRU_WIDTH // LRU_HEADS
LRU_C = 8.0
D_FF = 4 * D_MODEL
PLE_DIM = 256
EPS = 1e-6

U_GATES = 0
U_MOBA = 3 * D_MODEL
U_GDN = U_MOBA + 3 * MOBA_WIDTH
U_Z = U_GDN + GDN_CONV_DIM
U_X = U_Z + GDN_WIDTH
U_Y = U_X + LRU_WIDTH
U_COLS = U_Y + LRU_WIDTH
AB_COLS = 128

VMEM_LIMIT = 56 * 1024 * 1024
LANES = 128


def _params(*sem):
    return pltpu.CompilerParams(dimension_semantics=sem, vmem_limit_bytes=VMEM_LIMIT)


def _dot(a, b):
    return jnp.dot(a, b, preferred_element_type=F32)


def _dot_nt(a, b):
    return lax.dot_general(a, b, (((1,), (1,)), ((), ())), preferred_element_type=F32)


def _dot_tn(a, b):
    return lax.dot_general(a, b, (((0,), (0,)), ((), ())), preferred_element_type=F32)


def _split2(x):
    hi = x.astype(BF16)
    lo = (x - hi.astype(F32)).astype(BF16)
    return hi, lo


def _split3(x):
    hi = x.astype(BF16)
    r = x - hi.astype(F32)
    mid = r.astype(BF16)
    lo = (r - mid.astype(F32)).astype(BF16)
    return hi, mid, lo


def _dot_hp(a, b):
    ah, al = _split2(a)
    bh, bl = _split2(b)
    return _dot(ah, bh) + (_dot(ah, bl) + _dot(al, bh))


def _dot_lhs_exact(m_bf16, x):
    h, m, l = _split3(x)
    return _dot(m_bf16, h) + (_dot(m_bf16, m) + _dot(m_bf16, l))


def _dot_rhs_exact(x, m_bf16):
    h, m, l = _split3(x)
    return _dot(h, m_bf16) + (_dot(m, m_bf16) + _dot(l, m_bf16))


def _rms(xf, gain):
    ms = jnp.mean(xf * xf, axis=-1, keepdims=True)
    return xf * lax.rsqrt(ms + EPS) * gain


def _sigmoid(x):
    return 1.0 / (1.0 + jnp.exp(-x))


def _silu(x):
    return x * _sigmoid(x)


def _softplus(x):
    return jnp.maximum(x, 0.0) + jnp.log1p(jnp.exp(-jnp.abs(x)))


def _expm1(x):
    u = jnp.exp(x)
    um1 = u - 1.0
    lu = jnp.log(u)
    near = um1 * x / jnp.where(lu == 0.0, 1.0, lu)
    near = jnp.where(um1 == 0.0, x, near)
    return jnp.where(jnp.abs(x) < 0.5, near, um1)


def _norm_matmul_kernel(x_ref, g_ref, w_ref, o_ref, h_ref):
    @pl.when(pl.program_id(1) == 0)
    def _():
        h_ref[...] = _rms(x_ref[...], g_ref[...]).astype(BF16)

    o_ref[...] = _dot(h_ref[...], w_ref[...])


def norm_matmul(x, gain, w, layer, tm, tn):
    m, d = x.shape
    n = w.shape[-1]
    return pl.pallas_call(
        _norm_matmul_kernel,
        out_shape=jax.ShapeDtypeStruct((m, n), F32),
        grid=(m // tm, n // tn),
        in_specs=[
            pl.BlockSpec((tm, d), lambda i, j: (i, 0)),
            pl.BlockSpec((None, 1, d), lambda i, j: (layer, 0, 0)),
            pl.BlockSpec((None, d, tn), lambda i, j: (layer, 0, j)),
        ],
        out_specs=pl.BlockSpec((tm, tn), lambda i, j: (i, j)),
        scratch_shapes=[pltpu.VMEM((tm, d), BF16)],
        compiler_params=_params("parallel", "arbitrary"),
        name="norm_matmul",
    )(x, gain, w)


def _moba_prep_kernel(qkv_ref, cos_ref, sin_ref, bd_ref, gq_ref, gk_ref, q_ref, k_ref, v_ref, *hm_refs):
    cos = cos_ref[...]
    sin = sin_ref[...]
    bd = bd_ref[...]
    lane = lax.broadcasted_iota(jnp.int32, cos.shape, 1)
    first_half = (lane % MOBA_HEAD_DIM) < (MOBA_HEAD_DIM // 2)

    def norm_rot(x, gain):
        ms = _dot_rhs_exact(x * x, bd)
        y = x * lax.rsqrt(ms + EPS) * gain
        partner = jnp.where(first_half,
                            pltpu.roll(y, MOBA_WIDTH - MOBA_HEAD_DIM // 2, 1),
                            pltpu.roll(y, MOBA_HEAD_DIM // 2, 1))
        return y * cos + partner * sin

    q = norm_rot(qkv_ref[:, 0:MOBA_WIDTH], gq_ref[...])
    k = norm_rot(qkv_ref[:, MOBA_WIDTH:2 * MOBA_WIDTH], gk_ref[...])
    v = qkv_ref[:, 2 * MOBA_WIDTH:3 * MOBA_WIDTH]
    q_ref[...] = q
    k_ref[...] = k
    v_ref[...] = v
    if hm_refs:
        qh_ref, kh_ref, vh_ref = hm_refs
        for h in range(MOBA_HEADS):
            sl = slice(h * MOBA_HEAD_DIM, (h + 1) * MOBA_HEAD_DIM)
            qh_ref[0, h] = q[:, sl].astype(BF16)
            kh_ref[0, h] = k[:, sl].astype(BF16)
            vh_ref[0, h] = v[:, sl].astype(BF16)


def moba_prep(u, cos, sin, bd, gq, gk, layer, tq, seq_blocks, head_major):
    m = u.shape[0]
    n_steps = m // tq
    out_shape = [jax.ShapeDtypeStruct((m, MOBA_WIDTH), F32)] * 3
    out_specs = [pl.BlockSpec((tq, MOBA_WIDTH), lambda i: (i, 0))] * 3
    if head_major:
        nb = n_steps // seq_blocks
        hm = jax.ShapeDtypeStruct((nb, MOBA_HEADS, seq_blocks * tq, MOBA_HEAD_DIM), BF16)
        out_shape += [hm] * 3
        out_specs += [pl.BlockSpec((1, MOBA_HEADS, tq, MOBA_HEAD_DIM),
                                   lambda i: (i // seq_blocks, 0, i % seq_blocks, 0))] * 3
    tab = pl.BlockSpec((tq, MOBA_WIDTH), lambda i: (i % seq_blocks, 0))
    vec = pl.BlockSpec((None, 1, MOBA_WIDTH), lambda i: (layer, 0, 0))
    return pl.pallas_call(
        _moba_prep_kernel,
        out_shape=out_shape,
        grid=(n_steps,),
        in_specs=[
            pl.BlockSpec((tq, 3 * MOBA_WIDTH), lambda i: (i, U_MOBA // (3 * MOBA_WIDTH))),
            tab, tab,
            pl.BlockSpec((MOBA_WIDTH, MOBA_WIDTH), lambda i: (0, 0)),
            vec, vec,
        ],
        out_specs=out_specs,
        compiler_params=_params("parallel"),
        name="moba_prep",
    )(u, cos, sin, bd, gq, gk)


def _topk_select(gates):
    n = len(gates)
    if n <= MOBA_TOPK:
        return [None] * n
    sel = []
    for a in range(n):
        rank = jnp.zeros(gates[a].shape, F32)
        for b in range(n):
            if b == a:
                continue
            beats = (gates[b] >= gates[a]) if b < a else (gates[b] > gates[a])
            rank = rank + jnp.where(beats, 1.0, 0.0)
        sel.append(rank < float(MOBA_TOPK))
    return sel


def _moba_prompt_kernel(q_ref, k_ref, v_ref, o_ref, *, n_blk):
    blk = MOBA_BLOCK
    scale = MOBA_HEAD_DIM ** -0.5
    row = lax.broadcasted_iota(jnp.int32, (blk, blk), 0)
    col = lax.broadcasted_iota(jnp.int32, (blk, blk), 1)
    causal = row >= col
    for i in range(n_blk):
        q = q_ref[0, 0, i * blk:(i + 1) * blk, :]
        k = k_ref[0, 0, 0:(i + 1) * blk, :]
        v = v_ref[0, 0, 0:(i + 1) * blk, :]
        s = _dot_nt(q, k)
        parts = [s[:, j * blk:(j + 1) * blk] for j in range(i + 1)]
        gates = [jnp.sum(parts[j], axis=-1, keepdims=True) for j in range(i)]
        sel = _topk_select(gates)
        masked = []
        for j in range(i):
            masked.append(parts[j] if sel[j] is None else jnp.where(sel[j], parts[j], -jnp.inf))
        masked.append(jnp.where(causal, parts[i], -jnp.inf))
        mx = masked[0].max(axis=-1, keepdims=True)
        for j in range(1, i + 1):
            mx = jnp.maximum(mx, masked[j].max(axis=-1, keepdims=True))
        den = jnp.zeros((blk, 1), F32)
        acc = jnp.zeros((blk, MOBA_HEAD_DIM), F32)
        for j in range(i + 1):
            p = jnp.exp((masked[j] - mx) * scale)
            den = den + jnp.sum(p, axis=-1, keepdims=True)
            acc = acc + _dot(p.astype(BF16), v[j * blk:(j + 1) * blk, :])
        o_ref[0, 0, i * blk:(i + 1) * blk, :] = (acc / den).astype(BF16)


def moba_prompt(qh, kh, vh):
    nb, nh, t, dh = qh.shape
    spec = pl.BlockSpec((1, 1, t, dh), lambda b, h: (b, h, 0, 0))
    return pl.pallas_call(
        functools.partial(_moba_prompt_kernel, n_blk=t // MOBA_BLOCK),
        out_shape=jax.ShapeDtypeStruct((nb, nh, t, dh), BF16),
        grid=(nb, nh),
        in_specs=[spec, spec, spec],
        out_specs=spec,
        compiler_params=_params("parallel", "parallel"),
        name="moba_prompt",
    )(qh, kh, vh)


def _moba_sample_kernel(pt_ref, q_ref, kn_ref, vn_ref, *refs, n_pages, ppg, t_new):
    del pt_ref
    k_refs, v_refs = refs[:ppg], refs[ppg:2 * ppg]
    o_ref, s_all, v_all = refs[2 * ppg:]
    g = pl.program_id(1)
    rows = t_new * MOBA_HEADS
    cols = PAGE_SIZE * MOBA_HEADS
    scale = MOBA_HEAD_DIM ** -0.5
    q16 = q_ref[0].astype(BF16)
    for j in range(ppg):
        k_page = k_refs[j][...].reshape(cols, MOBA_HEAD_DIM)
        s_all[g * ppg + j] = _dot_nt(q16, k_page.astype(BF16))
        v_all[g * ppg + j] = v_refs[j][...].reshape(cols, MOBA_HEAD_DIM).astype(BF16)

    @pl.when(g == n_pages // ppg - 1)
    def _():
        pages_per_blk = MOBA_BLOCK // PAGE_SIZE
        n_past = n_pages // pages_per_blk
        same_head = (lax.broadcasted_iota(jnp.int32, (rows, cols), 1) % MOBA_HEADS
                     == lax.broadcasted_iota(jnp.int32, (rows, cols), 0) % MOBA_HEADS)
        pages = [jnp.where(same_head, s_all[j], -jnp.inf) for j in range(n_pages)]
        gates = []
        for n in range(n_past):
            gsum = jnp.zeros((rows, 1), F32)
            for r in range(pages_per_blk):
                gsum = gsum + jnp.sum(jnp.where(same_head, s_all[n * pages_per_blk + r], 0.0), axis=-1, keepdims=True)
            gates.append(gsum)
        sel = _topk_select(gates)
        masked = []
        for j in range(n_pages):
            s_n = sel[j // pages_per_blk]
            masked.append(pages[j] if s_n is None else jnp.where(s_n, pages[j], -jnp.inf))
        own_ok = ((lax.broadcasted_iota(jnp.int32, (rows, rows), 1) % MOBA_HEADS
                   == lax.broadcasted_iota(jnp.int32, (rows, rows), 0) % MOBA_HEADS)
                  & (lax.broadcasted_iota(jnp.int32, (rows, rows), 1) // MOBA_HEADS
                     <= lax.broadcasted_iota(jnp.int32, (rows, rows), 0) // MOBA_HEADS))
        own = jnp.where(own_ok, _dot_nt(q16, kn_ref[0].astype(BF16)), -jnp.inf)
        mx = own.max(axis=-1, keepdims=True)
        for j in range(n_pages):
            mx = jnp.maximum(mx, masked[j].max(axis=-1, keepdims=True))
        p_own = jnp.exp((own - mx) * scale)
        den = jnp.sum(p_own, axis=-1, keepdims=True)
        acc = _dot(p_own.astype(BF16), vn_ref[0].astype(BF16))
        for j in range(n_pages):
            pj = jnp.exp((masked[j] - mx) * scale)
            den = den + jnp.sum(pj, axis=-1, keepdims=True)
            acc = acc + _dot(pj.astype(BF16), v_all[j])
        o_ref[0] = acc / den


def moba_sample(page_table, q, k_new, v_new, cache_k, cache_v, layer, ppg):
    nb, rows, dh = q.shape
    n_pages = page_table.shape[1]
    tok = pl.BlockSpec((1, rows, dh), lambda b, g, pt: (b, 0, 0))

    def page(j):
        return pl.BlockSpec((None, PAGE_SIZE, None, MOBA_HEADS, dh),
                            lambda b, g, pt: (pt[b * n_pages + g * ppg + j], 0, layer, 0, 0))

    pages = [page(j) for j in range(ppg)]
    return pl.pallas_call(
        functools.partial(_moba_sample_kernel, n_pages=n_pages, ppg=ppg, t_new=rows // MOBA_HEADS),
        out_shape=jax.ShapeDtypeStruct((nb, rows, dh), F32),
        grid_spec=pltpu.PrefetchScalarGridSpec(
            num_scalar_prefetch=1,
            grid=(nb, n_pages // ppg),
            in_specs=[tok, tok, tok] + pages + pages,
            out_specs=tok,
            scratch_shapes=[pltpu.VMEM((n_pages, rows, PAGE_SIZE * MOBA_HEADS), F32),
                            pltpu.VMEM((n_pages, PAGE_SIZE * MOBA_HEADS, dh), BF16)],
        ),
        compiler_params=_params("parallel", "arbitrary"),
        name="moba_sample",
    )(page_table.reshape(-1), q, k_new, v_new, *([cache_k] * ppg), *([cache_v] * ppg))


def _unit_lower_inverse(lows, c):
    row = lax.broadcasted_iota(jnp.int32, (c, c), 0)
    col = lax.broadcasted_iota(jnp.int32, (c, c), 1)
    eye = jnp.where(row == col, 1.0, 0.0)
    pair = row // 2 == col // 2
    xs = [eye - jnp.where(pair, low, 0.0) for low in lows]
    s = 2
    while s < c:
        sub = (row // (2 * s) == col // (2 * s)) & (row // s != col // s)
        xe = [_dot_hp(x, jnp.where(sub, low, 0.0)) for x, low in zip(xs, lows)]
        xs = [x - _dot_hp(t, x) for x, t in zip(xs, xe)]
        s *= 2
    return xs


def _gdn_kernel(qkv_ref, z_ref, ab_ref, cw_ref, alog_ref, dtb_ref, gn_ref, cbuf_ref, s0_ref,
                o_ref, cnew_ref, snew_ref, xx_ref, st_ref, *, bb, c, t_valid):
    ci = pl.program_id(1)

    @pl.when(ci == 0)
    def _():
        xx_ref[:, 0:8, :] = cbuf_ref[...]
        st_ref[...] = s0_ref[...]

    xx_ref[:, 8:8 + c, :] = qkv_ref[...]
    w = cw_ref[...]
    masked = t_valid < c
    valid = lax.broadcasted_iota(jnp.int32, (c, 1), 0) < t_valid
    row = lax.broadcasted_iota(jnp.int32, (c, c), 0)
    col = lax.broadcasted_iota(jnp.int32, (c, c), 1)
    tri = row >= col
    eye = row == col
    tri16 = jnp.where(tri, 1.0, 0.0).astype(BF16)
    gn = gn_ref[...]

    ys, gcs, betas = [], [], []
    for bi in range(bb):
        y = (w[0:1, :] * xx_ref[bi, 5:5 + c, :] + w[1:2, :] * xx_ref[bi, 6:6 + c, :]
             + w[2:3, :] * xx_ref[bi, 7:7 + c, :] + w[3:4, :] * xx_ref[bi, 8:8 + c, :])
        cnew_ref[bi] = xx_ref[bi, 5 + t_valid:8 + t_valid, :]
        xx_ref[bi, 0:8, :] = xx_ref[bi, c:c + 8, :]
        ys.append(_silu(y))
        ab = ab_ref[bi]
        g_all = -jnp.exp(alog_ref[...]) * _softplus(ab + dtb_ref[...])
        if masked:
            g_all = jnp.where(valid, g_all, 0.0)
        gcs.append(_dot_lhs_exact(tri16, g_all))
        betas.append(_sigmoid(ab))

    chains = [(bi, h) for bi in range(bb) for h in range(GDN_HEADS)]
    q_l, k_l, kb_l, vb_l, gcol_l, glast_l, decay_l = [], [], [], [], [], [], []
    for bi, h in chains:
        lo, hi = h * GDN_DK, (h + 1) * GDN_DK
        y = ys[bi]
        q = y[:, lo:hi]
        k = y[:, GDN_KEY_WIDTH + lo:GDN_KEY_WIDTH + hi]
        v = y[:, 2 * GDN_KEY_WIDTH + lo:2 * GDN_KEY_WIDTH + hi]
        q = q * lax.rsqrt(jnp.sum(q * q, axis=-1, keepdims=True) + EPS) * (GDN_DK ** -0.5)
        k = k * lax.rsqrt(jnp.sum(k * k, axis=-1, keepdims=True) + EPS)
        beta = betas[bi][:, GDN_HEADS + h:GDN_HEADS + h + 1]
        if masked:
            k = jnp.where(valid, k, 0.0)
            v = jnp.where(valid, v, 0.0)
            beta = jnp.where(valid, beta, 0.0)
        g_col = gcs[bi][:, h:h + 1]
        g_row = jnp.sum(jnp.where(eye, g_col, 0.0), axis=0, keepdims=True)
        q_l.append(q)
        k_l.append(k)
        kb_l.append(k * beta)
        vb_l.append(v * beta)
        gcol_l.append(g_col)
        glast_l.append(gcs[bi][c - 1:c, h:h + 1])
        decay_l.append(jnp.exp(jnp.where(tri, g_col - g_row, -jnp.inf)))
    k16_l = [k.astype(BF16) for k in k_l]
    low_l = [jnp.where(row > col, _dot_nt(kb.astype(BF16), k16) * decay, 0.0)
             for kb, k16, decay in zip(kb_l, k16_l, decay_l)]
    a_l = [(_dot_nt(q.astype(BF16), k16) * decay).astype(BF16) for q, k16, decay in zip(q_l, k16_l, decay_l)]
    t_l = [t.astype(BF16) for t in _unit_lower_inverse(low_l, c)]
    eg_l = [jnp.exp(g) for g in gcol_l]
    u_l = [_dot(t, vb.astype(BF16)) for t, vb in zip(t_l, vb_l)]
    wk_l = [_dot(t, (kb * eg).astype(BF16)).astype(BF16) for t, kb, eg in zip(t_l, kb_l, eg_l)]
    qd_l = [(q * eg).astype(BF16) for q, eg in zip(q_l, eg_l)]
    kd_l = [(k * jnp.exp(gl - g)).astype(BF16) for k, gl, g in zip(k_l, glast_l, gcol_l)]
    st_l = [st_ref[bi, h] for bi, h in chains]
    st16_l = [st.astype(BF16) for st in st_l]
    vn_l = [(u - _dot(wk, st16)).astype(BF16) for u, wk, st16 in zip(u_l, wk_l, st16_l)]
    o_l = [_dot(qd, st16) + _dot(a, vn) for qd, st16, a, vn in zip(qd_l, st16_l, a_l, vn_l)]
    sn_l = [st * jnp.exp(gl) + _dot_tn(kd, vn) for st, gl, kd, vn in zip(st_l, glast_l, kd_l, vn_l)]
    for (bi, h), o, sn in zip(chains, o_l, sn_l):
        lo, hi = h * GDN_DK, (h + 1) * GDN_DK
        st_ref[bi, h] = sn
        snew_ref[bi, h] = sn
        on = _rms(o, gn) * _silu(z_ref[bi, :, lo:hi])
        o_ref[bi, :, lo:hi] = on.astype(BF16)


def gdn(u3, ab3, conv_w, a_log, dt_bias, out_norm, conv_buf, s0, layer, bb, c, t_valid):
    nb, t, _ = u3.shape
    vec = lambda n: pl.BlockSpec((None, 1, n), lambda b, i: (layer, 0, 0))
    return pl.pallas_call(
        functools.partial(_gdn_kernel, bb=bb, c=c, t_valid=t_valid),
        out_shape=[jax.ShapeDtypeStruct((nb, t, GDN_WIDTH), BF16),
                   jax.ShapeDtypeStruct((nb, CONV_WIDTH - 1, GDN_CONV_DIM), F32),
                   jax.ShapeDtypeStruct((nb, GDN_HEADS, GDN_DK, GDN_DV), F32)],
        grid=(nb // bb, t // c),
        in_specs=[
            pl.BlockSpec((bb, c, GDN_CONV_DIM), lambda b, i: (b, i, U_GDN // GDN_CONV_DIM)),
            pl.BlockSpec((bb, c, GDN_WIDTH), lambda b, i: (b, i, U_Z // GDN_WIDTH)),
            pl.BlockSpec((bb, c, AB_COLS), lambda b, i: (b, i, 0)),
            pl.BlockSpec((None, CONV_WIDTH, GDN_CONV_DIM), lambda b, i: (layer, 0, 0)),
            vec(AB_COLS), vec(AB_COLS), vec(GDN_DV),
            pl.BlockSpec((bb, 8, GDN_CONV_DIM), lambda b, i: (b, 0, 0)),
            pl.BlockSpec((bb, GDN_HEADS, GDN_DK, GDN_DV), lambda b, i: (b, 0, 0, 0)),
        ],
        out_specs=[
            pl.BlockSpec((bb, c, GDN_WIDTH), lambda b, i: (b, i, 0)),
            pl.BlockSpec((bb, CONV_WIDTH - 1, GDN_CONV_DIM), lambda b, i: (b, 0, 0)),
            pl.BlockSpec((bb, GDN_HEADS, GDN_DK, GDN_DV), lambda b, i: (b, 0, 0, 0)),
        ],
        scratch_shapes=[pltpu.VMEM((bb, c + 8, GDN_CONV_DIM), F32),
                        pltpu.VMEM((bb, GDN_HEADS, GDN_DK, GDN_DV), F32)],
        compiler_params=_params("parallel", "arbitrary"),
        name="gdn",
    )(u3, u3, ab3, conv_w, a_log, dt_bias, out_norm, conv_buf, s0)


def _lru_gates(xf, wa, ba, wx, bx, lam):
    x16 = xf.astype(BF16)
    r = _sigmoid(_dot(x16, wa) + ba)
    i = _sigmoid(_dot(x16, wx) + bx)
    log_a = -LRU_C * r * _softplus(-lam)
    a = jnp.exp(log_a)
    b = jnp.sqrt(-_expm1(2.0 * log_a)) * (i * xf)
    return a, b


def _lru_prompt_kernel(x_ref, y_ref, cw_ref, cb_ref, wa_ref, ba_ref, wx_ref, bx_ref, lam_ref, cbuf_ref, h0_ref,
                       o_ref, cnew_ref, hlast_ref, xx_ref, a_s, b_s, h_s, hcar, *, nb, tc):
    ci = pl.program_id(0)

    @pl.when(ci == 0)
    def _():
        xx_ref[:, 0:8, :] = cbuf_ref[...]
        hcar[...] = h0_ref[...]

    xx_ref[:, 8:8 + tc, :] = x_ref[...]
    w = cw_ref[...]
    xc = (w[0:1, :] * xx_ref[:, 5:5 + tc, :] + w[1:2, :] * xx_ref[:, 6:6 + tc, :]
          + w[2:3, :] * xx_ref[:, 7:7 + tc, :] + w[3:4, :] * xx_ref[:, 8:8 + tc, :]) + cb_ref[...]
    cnew_ref[...] = xx_ref[:, tc + 5:tc + 8, :]
    xx_ref[:, 0:8, :] = xx_ref[:, tc:tc + 8, :]
    xf = xc.reshape(nb * tc, LRU_WIDTH)
    a, b = _lru_gates(xf, wa_ref[...], ba_ref[...], wx_ref[...], bx_ref[...], lam_ref[...])
    n_lane_tiles = LRU_WIDTH // LANES
    for j in range(n_lane_tiles):
        a_s[j] = a[:, j * LANES:(j + 1) * LANES]
        b_s[j] = b[:, j * LANES:(j + 1) * LANES]

    def step(t, hs):
        new = []
        for j in range(n_lane_tiles):
            h = a_s[j, pl.ds(t, nb, stride=tc), :] * hs[j] + b_s[j, pl.ds(t, nb, stride=tc), :]
            h_s[j, pl.ds(t, nb, stride=tc), :] = h
            new.append(h)
        return tuple(new)

    h0 = hcar[...]
    hs = lax.fori_loop(0, tc, step, tuple(h0[:, j * LANES:(j + 1) * LANES] for j in range(n_lane_tiles)))
    h = jnp.concatenate(hs, axis=1)
    hcar[...] = h
    hlast_ref[...] = h
    h_all = jnp.concatenate([h_s[j] for j in range(n_lane_tiles)], axis=1)
    out = h_all * jax.nn.gelu(y_ref[...].reshape(nb * tc, LRU_WIDTH))
    o_ref[...] = out.reshape(nb, tc, LRU_WIDTH).astype(BF16)


def lru_prompt(u3, conv_w, conv_b, wa, ba, wx, bx, lam, conv_buf, h0, layer, tc):
    nb, t, _ = u3.shape
    vec = pl.BlockSpec((None, 1, LRU_WIDTH), lambda i: (layer, 0, 0))
    mat = pl.BlockSpec((None, LRU_WIDTH, LRU_WIDTH), lambda i: (layer, 0, 0))
    return pl.pallas_call(
        functools.partial(_lru_prompt_kernel, nb=nb, tc=tc),
        out_shape=[jax.ShapeDtypeStruct((nb, t, LRU_WIDTH), BF16),
                   jax.ShapeDtypeStruct((nb, CONV_WIDTH - 1, LRU_WIDTH), F32),
                   jax.ShapeDtypeStruct((nb, LRU_WIDTH), F32)],
        grid=(t // tc,),
        in_specs=[
            pl.BlockSpec((nb, tc, LRU_WIDTH), lambda i: (0, i, U_X // LRU_WIDTH)),
            pl.BlockSpec((nb, tc, LRU_WIDTH), lambda i: (0, i, U_Y // LRU_WIDTH)),
            pl.BlockSpec((None, CONV_WIDTH, LRU_WIDTH), lambda i: (layer, 0, 0)),
            vec, mat, vec, mat, vec, vec,
            pl.BlockSpec((nb, 8, LRU_WIDTH), lambda i: (0, 0, 0)),
            pl.BlockSpec((nb, LRU_WIDTH), lambda i: (0, 0)),
        ],
        out_specs=[
            pl.BlockSpec((nb, tc, LRU_WIDTH), lambda i: (0, i, 0)),
            pl.BlockSpec((nb, CONV_WIDTH - 1, LRU_WIDTH), lambda i: (0, 0, 0)),
            pl.BlockSpec((nb, LRU_WIDTH), lambda i: (0, 0)),
        ],
        scratch_shapes=[pltpu.VMEM((nb, tc + 8, LRU_WIDTH), F32),
                        pltpu.VMEM((LRU_WIDTH // LANES, nb * tc, LANES), F32),
                        pltpu.VMEM((LRU_WIDTH // LANES, nb * tc, LANES), F32),
                        pltpu.VMEM((LRU_WIDTH // LANES, nb * tc, LANES), F32),
                        pltpu.VMEM((nb, LRU_WIDTH), F32)],
        compiler_params=_params("arbitrary"),
        name="lru_prompt",
    )(u3, u3, conv_w, conv_b, wa, ba, wx, bx, lam, conv_buf, h0)


def _lru_sample_kernel(x_ref, y_ref, cw_ref, cb_ref, wa_ref, ba_ref, wx_ref, bx_ref, lam_ref, cbuf_ref, h0_ref,
                       o_ref, cnew_ref, hlast_ref, *, t_new):
    w = cw_ref[...]
    xx = [cbuf_ref[j] for j in range(CONV_WIDTH - 1)] + [x_ref[j] for j in range(t_new)]
    for j in range(CONV_WIDTH - 1):
        cnew_ref[j] = xx[t_new + j]
    h = h0_ref[...]
    for t in range(t_new):
        xf = cb_ref[...] + w[0:1, :] * xx[t]
        for j in range(1, CONV_WIDTH):
            xf = xf + w[j:j + 1, :] * xx[t + j]
        a, b = _lru_gates(xf, wa_ref[...], ba_ref[...], wx_ref[...], bx_ref[...], lam_ref[...])
        h = a * h + b
        o_ref[t] = (h * jax.nn.gelu(y_ref[t])).astype(BF16)
    hlast_ref[...] = h


def lru_sample(x_tm, y_tm, conv_w, conv_b, wa, ba, wx, bx, lam, conv_buf_tm, h0, layer):
    t_new, nb, _ = x_tm.shape
    vec = pl.BlockSpec((None, 1, LRU_WIDTH), lambda i: (layer, 0, 0))
    mat = pl.BlockSpec((None, LRU_WIDTH, LRU_WIDTH), lambda i: (layer, 0, 0))
    tok = pl.BlockSpec((t_new, nb, LRU_WIDTH), lambda i: (0, 0, 0))
    buf = pl.BlockSpec((CONV_WIDTH - 1, nb, LRU_WIDTH), lambda i: (0, 0, 0))
    st = pl.BlockSpec((nb, LRU_WIDTH), lambda i: (0, 0))
    return pl.pallas_call(
        functools.partial(_lru_sample_kernel, t_new=t_new),
        out_shape=[jax.ShapeDtypeStruct((t_new, nb, LRU_WIDTH), BF16),
                   jax.ShapeDtypeStruct((CONV_WIDTH - 1, nb, LRU_WIDTH), F32),
                   jax.ShapeDtypeStruct((nb, LRU_WIDTH), F32)],
        grid=(1,),
        in_specs=[tok, tok, pl.BlockSpec((None, CONV_WIDTH, LRU_WIDTH), lambda i: (layer, 0, 0)),
                  vec, mat, vec, mat, vec, vec, buf, st],
        out_specs=[tok, buf, st],
        compiler_params=_params("arbitrary"),
        name="lru_sample",
    )(x_tm, y_tm, conv_w, conv_b, wa, ba, wx, bx, lam, conv_buf_tm, h0)


def _merge_kernel(oa_ref, od_ref, or_ref, ga_ref, gd_ref, gr_ref, x_ref, wa_ref, wd_ref, wr_ref, wo_ref, o_ref):
    merged = (_sigmoid(ga_ref[...]) * _dot(oa_ref[...], wa_ref[...])
              + _sigmoid(gd_ref[...]) * _dot(od_ref[...], wd_ref[...])
              + _sigmoid(gr_ref[...]) * _dot(or_ref[...], wr_ref[...]))
    o_ref[...] = x_ref[...] + _dot(merged.astype(BF16), wo_ref[...])


def merge(o_a, o_d, o_r, u, x, w_a, w_d, w_r, w_out, layer, tm):
    m, d = x.shape
    br = lambda n: pl.BlockSpec((tm, n), lambda i: (i, 0))
    gate = lambda j: pl.BlockSpec((tm, d), lambda i: (i, j))
    wspec = lambda k: pl.BlockSpec((None, k, d), lambda i: (layer, 0, 0))
    return pl.pallas_call(
        _merge_kernel,
        out_shape=jax.ShapeDtypeStruct((m, d), F32),
        grid=(m // tm,),
        in_specs=[br(MOBA_WIDTH), br(GDN_WIDTH), br(LRU_WIDTH), gate(0), gate(1), gate(2), br(d),
                  wspec(MOBA_WIDTH), wspec(GDN_WIDTH), wspec(LRU_WIDTH), wspec(d)],
        out_specs=br(d),
        compiler_params=_params("parallel"),
        name="merge",
    )(o_a, o_d, o_r, u, u, u, x, w_a, w_d, w_r, w_out)


def _ffn_kernel(x_ref, g_ref, wu_ref, wd_ref, o_ref, h_ref, acc_ref):
    f = pl.program_id(1)

    @pl.when(f == 0)
    def _():
        h_ref[...] = _rms(x_ref[...], g_ref[...]).astype(BF16)
        acc_ref[...] = jnp.zeros_like(acc_ref)

    a = jnp.maximum(_dot(h_ref[...], wu_ref[...]), 0.0)
    acc_ref[...] += _dot((a * a).astype(BF16), wd_ref[...])

    @pl.when(f == pl.num_programs(1) - 1)
    def _():
        o_ref[...] = x_ref[...] + acc_ref[...]


def ffn(x, gain, w_up, w_down, layer, tm, tf):
    m, d = x.shape
    f = w_up.shape[-1]
    return pl.pallas_call(
        _ffn_kernel,
        out_shape=jax.ShapeDtypeStruct((m, d), F32),
        grid=(m // tm, f // tf),
        in_specs=[
            pl.BlockSpec((tm, d), lambda i, j: (i, 0)),
            pl.BlockSpec((None, 1, d), lambda i, j: (layer, 0, 0)),
            pl.BlockSpec((None, d, tf), lambda i, j: (layer, 0, j)),
            pl.BlockSpec((None, tf, d), lambda i, j: (layer, j, 0)),
        ],
        out_specs=pl.BlockSpec((tm, d), lambda i, j: (i, 0)),
        scratch_shapes=[pltpu.VMEM((tm, d), BF16), pltpu.VMEM((tm, d), F32)],
        compiler_params=_params("parallel", "arbitrary"),
        name="ffn",
    )(x, gain, w_up, w_down)


def _ple_kernel(x_ref, g_ref, wg_ref, p_ref, wp_ref, o_ref):
    x = x_ref[...]
    gate = _sigmoid(_dot(_rms(x, g_ref[...]).astype(BF16), wg_ref[...]))
    o_ref[...] = x + gate * _dot(p_ref[...].astype(BF16), wp_ref[...])


def ple(x, gain, w_gate, p, w_proj, layer, tm):
    m, d = x.shape
    steps = m // tm
    return pl.pallas_call(
        _ple_kernel,
        out_shape=jax.ShapeDtypeStruct((m, d), F32),
        grid=(steps,),
        in_specs=[
            pl.BlockSpec((tm, d), lambda i: (i, 0)),
            pl.BlockSpec((None, 1, d), lambda i: (layer, 0, 0)),
            pl.BlockSpec((None, d, d), lambda i: (layer, 0, 0)),
            pl.BlockSpec((tm, PLE_DIM), lambda i: (layer * steps + i, 0)),
            pl.BlockSpec((None, PLE_DIM, d), lambda i: (layer, 0, 0)),
        ],
        out_specs=pl.BlockSpec((tm, d), lambda i: (i, 0)),
        compiler_params=_params("parallel"),
        name="ple",
    )(x, gain, w_gate, p, w_proj)


def _rope_tables(pos):
    half = MOBA_HEAD_DIM // 2
    inv_freq = ROPE_THETA ** (-jnp.arange(half, dtype=F32) / half)
    ang = pos.astype(F32)[:, None] * inv_freq[None, :]
    cos = jnp.cos(ang)
    sin = jnp.sin(ang)
    cos_h = jnp.concatenate([cos, cos], axis=-1)
    sin_h = jnp.concatenate([-sin, sin], axis=-1)
    return jnp.tile(cos_h, (1, MOBA_HEADS)), jnp.tile(sin_h, (1, MOBA_HEADS))


def _block_diag(w):
    l, h, n, _ = w.shape
    eye = jnp.eye(h, dtype=w.dtype)
    return (w[:, :, :, None, :] * eye[None, :, None, :, None]).reshape(l, h * n, h * n)


def _row3(v):
    return v[:, None, :]


def _pad_lanes(v, n):
    return jnp.pad(v, ((0, 0), (0, n - v.shape[-1])))[:, None, :]


def kernel(x_prompt, x_sample, cache_k, cache_v, state_gdn, state_gdn_conv, state_lru_h, state_lru_conv,
           page_table, p_prompt, p_sample, g_mix, w_in, moba_q_norm, moba_k_norm, w_branch_a, gdn_conv_w,
           gdn_a_log, gdn_dt_bias, gdn_out_norm, w_branch_d, lru_conv_w, lru_conv_b, lru_wa, lru_ba, lru_wx,
           lru_bx, lru_lambda, w_branch_r, w_out, g_ffn, w_up, w_down, g_ple, w_ple_gate, w_ple_proj):
    depth = w_in.shape[0]
    bp, seq, d = x_prompt.shape
    bs, t_new, _ = x_sample.shape
    n_pages = page_table.shape[1]
    past_len = n_pages * PAGE_SIZE
    mp, ms = bp * seq, bs * t_new

    o = 0
    offs = []
    for size in (3 * MOBA_WIDTH, GDN_CONV_DIM, GDN_WIDTH, GDN_HEADS, GDN_HEADS, LRU_WIDTH, LRU_WIDTH, 3 * D_MODEL):
        offs.append((o, o + size))
        o += size
    (m0, m1), (d0, d1), (z0, z1), (a0, a1), (b0, b1), (x0, x1), (y0, y1), (g0, g1) = offs
    w_main = jnp.concatenate([w_in[:, :, g0:g1], w_in[:, :, m0:m1], w_in[:, :, d0:d1], w_in[:, :, z0:z1],
                              w_in[:, :, x0:x1], w_in[:, :, y0:y1]], axis=-1).astype(BF16)
    w_ab = jnp.pad(w_in[:, :, a0:b1], ((0, 0), (0, 0), (0, AB_COLS - 2 * GDN_HEADS))).astype(BF16)
    w_a16, w_d16, w_r16, w_o16 = (w.astype(BF16) for w in (w_branch_a, w_branch_d, w_branch_r, w_out))
    w_up16, w_down16, w_pg16, w_pp16 = (w.astype(BF16) for w in (w_up, w_down, w_ple_gate, w_ple_proj))
    wa_bd = _block_diag(lru_wa).astype(BF16)
    wx_bd = _block_diag(lru_wx).astype(BF16)
    head_mean = jnp.kron(jnp.eye(MOBA_HEADS, dtype=F32),
                         jnp.full((MOBA_HEAD_DIM, MOBA_HEAD_DIM), 1.0 / MOBA_HEAD_DIM, F32)).astype(BF16)
    gq = _row3(jnp.tile(moba_q_norm, (1, MOBA_HEADS)))
    gk = _row3(jnp.tile(moba_k_norm, (1, MOBA_HEADS)))
    g_mix3, g_ffn3, g_ple3 = _row3(g_mix), _row3(g_ffn), _row3(g_ple)
    a_log3 = _pad_lanes(gdn_a_log, AB_COLS)
    dt_bias3 = _pad_lanes(gdn_dt_bias, AB_COLS)
    out_norm3 = _row3(gdn_out_norm)
    lru_cb3, lru_ba3, lru_bx3, lru_lam3 = _row3(lru_conv_b), _row3(lru_ba), _row3(lru_bx), _row3(lru_lambda)

    cos_p, sin_p = _rope_tables(jnp.arange(seq, dtype=jnp.int32))
    cos_s, sin_s = _rope_tables(past_len + jnp.arange(t_new, dtype=jnp.int32))
    cos_s, sin_s = jnp.tile(cos_s, (bs, 1)), jnp.tile(sin_s, (bs, 1))

    pp = p_prompt.reshape(depth * mp, PLE_DIM)
    ps = p_sample.reshape(depth * ms, PLE_DIM)

    zero_gconv = jnp.zeros((bp, 8, GDN_CONV_DIM), F32)
    zero_gstate = jnp.zeros((bp, GDN_HEADS, GDN_DK, GDN_DV), F32)
    zero_lconv = jnp.zeros((bp, 8, LRU_WIDTH), F32)
    zero_lh = jnp.zeros((bp, LRU_WIDTH), F32)
    c_s = 16

    xp = x_prompt.reshape(mp, d)
    xs = x_sample.reshape(ms, d)
    outs = {k: [] for k in ("kp", "vp", "ks", "vs", "gsp", "gss", "gcp", "gcs", "lhp", "lhs", "lcp", "lcs")}

    for l in range(depth):
        u = norm_matmul(xp, g_mix3, w_main, l, 512, 1536)
        ab = norm_matmul(xp, g_mix3, w_ab, l, 512, AB_COLS)
        _, k_a, v_a, qh, kh, vh = moba_prep(u, cos_p, sin_p, head_mean, gq, gk, l, MOBA_BLOCK,
                                            seq // MOBA_BLOCK, True)
        o_a = moba_prompt(qh, kh, vh)
        o_a = o_a.transpose(0, 2, 1, 3).reshape(mp, MOBA_WIDTH)
        u3 = u.reshape(bp, seq, U_COLS)
        o_d, gconv, gstate = gdn(u3, ab.reshape(bp, seq, AB_COLS), gdn_conv_w, a_log3, dt_bias3, out_norm3,
                                 zero_gconv, zero_gstate, l, 2, GDN_CHUNK, GDN_CHUNK)
        o_r, lconv, lh = lru_prompt(u3, lru_conv_w, lru_cb3, wa_bd, lru_ba3, wx_bd, lru_bx3, lru_lam3,
                                    zero_lconv, zero_lh, l, 256)
        xp = merge(o_a, o_d.reshape(mp, GDN_WIDTH), o_r.reshape(mp, LRU_WIDTH), u, xp,
                   w_a16, w_d16, w_r16, w_o16, l, 256)
        xp = ffn(xp, g_ffn3, w_up16, w_down16, l, 512, 1024)
        xp = ple(xp, g_ple3, w_pg16, pp, w_pp16, l, 512)
        outs["kp"].append(k_a); outs["vp"].append(v_a); outs["gcp"].append(gconv); outs["gsp"].append(gstate)
        outs["lcp"].append(lconv); outs["lhp"].append(lh)

        u = norm_matmul(xs, g_mix3, w_main, l, ms, 1536)
        ab = norm_matmul(xs, g_mix3, w_ab, l, ms, AB_COLS)
        q_s, k_s, v_s = moba_prep(u, cos_s, sin_s, head_mean, gq, gk, l, ms, 1, False)
        rows_s = (bs, t_new * MOBA_HEADS, MOBA_HEAD_DIM)
        o_a = moba_sample(page_table, q_s.reshape(rows_s), k_s.reshape(rows_s), v_s.reshape(rows_s),
                          cache_k, cache_v, l, 4)
        o_a = o_a.reshape(ms, MOBA_WIDTH).astype(BF16)
        pad_t = ((0, 0), (0, c_s - t_new), (0, 0))
        u3 = jnp.pad(u.reshape(bs, t_new, U_COLS), pad_t)
        ab3 = jnp.pad(ab.reshape(bs, t_new, AB_COLS), pad_t)
        gbuf = jnp.pad(state_gdn_conv[:, l], ((0, 0), (8 - (CONV_WIDTH - 1), 0), (0, 0)))
        o_d, gconv, gstate = gdn(u3, ab3, gdn_conv_w, a_log3, dt_bias3, out_norm3,
                                 gbuf, state_gdn[:, l], l, 8, c_s, t_new)
        o_d = o_d[:, :t_new].reshape(ms, GDN_WIDTH)
        us = u.reshape(bs, t_new, U_COLS)
        x_tm = us[:, :, U_X:U_X + LRU_WIDTH].transpose(1, 0, 2)
        y_tm = us[:, :, U_Y:U_Y + LRU_WIDTH].transpose(1, 0, 2)
        o_r, lconv, lh = lru_sample(x_tm, y_tm, lru_conv_w, lru_cb3, wa_bd, lru_ba3, wx_bd, lru_bx3, lru_lam3,
                                    state_lru_conv[:, l].transpose(1, 0, 2), state_lru_h[:, l], l)
        o_r = o_r.transpose(1, 0, 2).reshape(ms, LRU_WIDTH)
        xs = merge(o_a, o_d, o_r, u, xs, w_a16, w_d16, w_r16, w_o16, l, 256)
        xs = ffn(xs, g_ffn3, w_up16, w_down16, l, ms, 1024)
        xs = ple(xs, g_ple3, w_pg16, ps, w_pp16, l, ms)
        outs["ks"].append(k_s); outs["vs"].append(v_s); outs["gcs"].append(gconv); outs["gss"].append(gstate)
        outs["lcs"].append(lconv.transpose(1, 0, 2)); outs["lhs"].append(lh)

    hd = (MOBA_HEADS, MOBA_HEAD_DIM)
    k_prompt = jnp.stack(outs["kp"], axis=1).reshape(bp, seq, depth, *hd)
    v_prompt = jnp.stack(outs["vp"], axis=1).reshape(bp, seq, depth, *hd)
    k_sample = jnp.stack(outs["ks"], axis=1).reshape(bs, t_new, depth, *hd)
    v_sample = jnp.stack(outs["vs"], axis=1).reshape(bs, t_new, depth, *hd)
    return (xp.reshape(bp, seq, d), xs.reshape(bs, t_new, d), k_prompt, v_prompt, k_sample, v_sample,
            jnp.stack(outs["gsp"], axis=1), jnp.stack(outs["gss"], axis=1),
            jnp.stack(outs["gcp"], axis=1), jnp.stack(outs["gcs"], axis=1),
            jnp.stack(outs["lhp"], axis=1), jnp.stack(outs["lhs"], axis=1),
            jnp.stack(outs["lcp"], axis=1), jnp.stack(outs["lcs"], axis=1))
```

```python
import functools
import math

import jax
import jax.numpy as jnp
from jax import lax
from jax.experimental import pallas as pl
from jax.experimental.pallas import tpu as pltpu

F32 = jnp.float32
BF16 = jnp.bfloat16

D_MODEL = 1024
MOBA_HEADS = 8
MOBA_HEAD_DIM = 64
MOBA_WIDTH = MOBA_HEADS * MOBA_HEAD_DIM
MOBA_BLOCK = 256
MOBA_TOPK = 3
ROPE_THETA = 10000.0
PAGE_SIZE = 128
GDN_HEADS = 4
GDN_DK = 128
GDN_DV = 128
GDN_KEY_WIDTH = GDN_HEADS * GDN_DK
GDN_WIDTH = GDN_HEADS * GDN_DV
GDN_CONV_DIM = 2 * GDN_KEY_WIDTH + GDN_WIDTH
GDN_CHUNK = 64
CONV_WIDTH = 4
LRU_WIDTH = 512
LRU_HEADS = 8
LRU_BLOCK = LRU_WIDTH // LRU_HEADS
LRU_C = 8.0
D_FF = 4 * D_MODEL
PLE_DIM = 256
EPS = 1e-6

U_GATES = 0
U_MOBA = 3 * D_MODEL
U_GDN = U_MOBA + 3 * MOBA_WIDTH
U_Z = U_GDN + GDN_CONV_DIM
U_X = U_Z + GDN_WIDTH
U_Y = U_X + LRU_WIDTH
U_COLS = U_Y + LRU_WIDTH
AB_COLS = 128

VMEM_LIMIT = 56 * 1024 * 1024
LANES = 128


def _params(*sem):
    return pltpu.CompilerParams(dimension_semantics=sem, vmem_limit_bytes=VMEM_LIMIT)


def _dot(a, b):
    return jnp.dot(a, b, preferred_element_type=F32)


def _dot_nt(a, b):
    return lax.dot_general(a, b, (((1,), (1,)), ((), ())), preferred_element_type=F32)


def _dot_tn(a, b):
    return lax.dot_general(a, b, (((0,), (0,)), ((), ())), preferred_element_type=F32)


def _split2(x):
    hi = x.astype(BF16)
    lo = (x - hi.astype(F32)).astype(BF16)
    return hi, lo


def _split3(x):
    hi = x.astype(BF16)
    r = x - hi.astype(F32)
    mid = r.astype(BF16)
    lo = (r - mid.astype(F32)).astype(BF16)
    return hi, mid, lo


def _dot_hp(a, b):
    ah, al = _split2(a)
    bh, bl = _split2(b)
    return _dot(ah, bh) + (_dot(ah, bl) + _dot(al, bh))


def _dot_lhs_exact(m_bf16, x):
    h, m, l = _split3(x)
    return _dot(m_bf16, h) + (_dot(m_bf16, m) + _dot(m_bf16, l))


def _dot_rhs_exact(x, m_bf16):
    h, m, l = _split3(x)
    return _dot(h, m_bf16) + (_dot(m, m_bf16) + _dot(l, m_bf16))


def _rms(xf, gain):
    ms = jnp.mean(xf * xf, axis=-1, keepdims=True)
    return xf * lax.rsqrt(ms + EPS) * gain


def _sigmoid(x):
    return 1.0 / (1.0 + jnp.exp(-x))


def _silu(x):
    return x * _sigmoid(x)


def _softplus(x):
    return jnp.maximum(x, 0.0) + jnp.log1p(jnp.exp(-jnp.abs(x)))


def _expm1(x):
    u = jnp.exp(x)
    um1 = u - 1.0
    lu = jnp.log(u)
    near = um1 * x / jnp.where(lu == 0.0, 1.0, lu)
    near = jnp.where(um1 == 0.0, x, near)
    return jnp.where(jnp.abs(x) < 0.5, near, um1)


def _norm_matmul_kernel(x_ref, g_ref, w_ref, w2_ref, o_ref, o2_ref, h_ref):
    @pl.when(pl.program_id(1) == 0)
    def _():
        h = _rms(x_ref[...], g_ref[...]).astype(BF16)
        h_ref[...] = h
        o2_ref[...] = _dot(h, w2_ref[...])

    o_ref[...] = _dot(h_ref[...], w_ref[...])


def norm_matmul(x, gain, w, w2, layer, tm, tn):
    m, d = x.shape
    n, n2 = w.shape[-1], w2.shape[-1]
    return pl.pallas_call(
        _norm_matmul_kernel,
        out_shape=[jax.ShapeDtypeStruct((m, n), F32), jax.ShapeDtypeStruct((m, n2), F32)],
        grid=(m // tm, n // tn),
        in_specs=[
            pl.BlockSpec((tm, d), lambda i, j: (i, 0)),
            pl.BlockSpec((None, 1, d), lambda i, j: (layer, 0, 0)),
            pl.BlockSpec((None, d, tn), lambda i, j: (layer, 0, j)),
            pl.BlockSpec((None, d, n2), lambda i, j: (layer, 0, 0)),
        ],
        out_specs=[pl.BlockSpec((tm, tn), lambda i, j: (i, j)),
                   pl.BlockSpec((tm, n2), lambda i, j: (i, 0))],
        scratch_shapes=[pltpu.VMEM((tm, d), BF16)],
        compiler_params=_params("parallel", "arbitrary"),
        name="norm_matmul",
    )(x, gain, w, w2)


def _moba_prep_kernel(qkv_ref, cos_ref, sin_ref, bd_ref, gq_ref, gk_ref, q_ref, k_ref, v_ref, *hm_refs):
    cos = cos_ref[...]
    sin = sin_ref[...]
    bd = bd_ref[...]
    lane = lax.broadcasted_iota(jnp.int32, cos.shape, 1)
    first_half = (lane % MOBA_HEAD_DIM) < (MOBA_HEAD_DIM // 2)

    def norm_rot(x, gain):
        ms = _dot_rhs_exact(x * x, bd)
        y = x * lax.rsqrt(ms + EPS) * gain
        partner = jnp.where(first_half,
                            pltpu.roll(y, MOBA_WIDTH - MOBA_HEAD_DIM // 2, 1),
                            pltpu.roll(y, MOBA_HEAD_DIM // 2, 1))
        return y * cos + partner * sin

    q = norm_rot(qkv_ref[:, 0:MOBA_WIDTH], gq_ref[...])
    k = norm_rot(qkv_ref[:, MOBA_WIDTH:2 * MOBA_WIDTH], gk_ref[...])
    v = qkv_ref[:, 2 * MOBA_WIDTH:3 * MOBA_WIDTH]
    q_ref[...] = q
    k_ref[...] = k
    v_ref[...] = v
    if hm_refs:
        qh_ref, kh_ref, vh_ref = hm_refs
        for h in range(MOBA_HEADS):
            sl = slice(h * MOBA_HEAD_DIM, (h + 1) * MOBA_HEAD_DIM)
            qh_ref[0, h] = q[:, sl].astype(BF16)
            kh_ref[0, h] = k[:, sl].astype(BF16)
            vh_ref[0, h] = v[:, sl].astype(BF16)


def moba_prep(u, cos, sin, bd, gq, gk, layer, tq, seq_blocks, head_major):
    m = u.shape[0]
    n_steps = m // tq
    out_shape = [jax.ShapeDtypeStruct((m, MOBA_WIDTH), F32)] * 3
    out_specs = [pl.BlockSpec((tq, MOBA_WIDTH), lambda i: (i, 0))] * 3
    if head_major:
        nb = n_steps // seq_blocks
        hm = jax.ShapeDtypeStruct((nb, MOBA_HEADS, seq_blocks * tq, MOBA_HEAD_DIM), BF16)
        out_shape += [hm] * 3
        out_specs += [pl.BlockSpec((1, MOBA_HEADS, tq, MOBA_HEAD_DIM),
                                   lambda i: (i // seq_blocks, 0, i % seq_blocks, 0))] * 3
    tab = pl.BlockSpec((tq, MOBA_WIDTH), lambda i: (i % seq_blocks, 0))
    vec = pl.BlockSpec((None, 1, MOBA_WIDTH), lambda i: (layer, 0, 0))
    return pl.pallas_call(
        _moba_prep_kernel,
        out_shape=out_shape,
        grid=(n_steps,),
        in_specs=[
            pl.BlockSpec((tq, 3 * MOBA_WIDTH), lambda i: (i, U_MOBA // (3 * MOBA_WIDTH))),
            tab, tab,
            pl.BlockSpec((MOBA_WIDTH, MOBA_WIDTH), lambda i: (0, 0)),
            vec, vec,
        ],
        out_specs=out_specs,
        compiler_params=_params("parallel"),
        name="moba_prep",
    )(u, cos, sin, bd, gq, gk)


def _topk_select(gates):
    n = len(gates)
    if n <= MOBA_TOPK:
        return [None] * n
    sel = []
    for a in range(n):
        rank = jnp.zeros(gates[a].shape, F32)
        for b in range(n):
            if b == a:
                continue
            beats = (gates[b] >= gates[a]) if b < a else (gates[b] > gates[a])
            rank = rank + jnp.where(beats, 1.0, 0.0)
        sel.append(rank < float(MOBA_TOPK))
    return sel


HEADS_PER_STEP = LANES // MOBA_HEAD_DIM


def _moba_prompt_kernel(q_ref, k_ref, v_ref, o_ref, *, n_blk):
    blk = MOBA_BLOCK
    scale = MOBA_HEAD_DIM ** -0.5
    row = lax.broadcasted_iota(jnp.int32, (blk, blk), 0)
    col = lax.broadcasted_iota(jnp.int32, (blk, blk), 1)
    causal = row >= col
    for i in range(n_blk):
        outs = []
        for hh in range(HEADS_PER_STEP):
            q = q_ref[0, hh, i * blk:(i + 1) * blk, :]
            k = k_ref[0, hh, 0:(i + 1) * blk, :]
            v = v_ref[0, hh, 0:(i + 1) * blk, :]
            s = _dot_nt(q, k)
            parts = [s[:, j * blk:(j + 1) * blk] for j in range(i + 1)]
            gates = [jnp.sum(parts[j], axis=-1, keepdims=True) for j in range(i)]
            sel = _topk_select(gates)
            masked = []
            for j in range(i):
                masked.append(parts[j] if sel[j] is None else jnp.where(sel[j], parts[j], -jnp.inf))
            masked.append(jnp.where(causal, parts[i], -jnp.inf))
            mx = masked[0].max(axis=-1, keepdims=True)
            for j in range(1, i + 1):
                mx = jnp.maximum(mx, masked[j].max(axis=-1, keepdims=True))
            den = jnp.zeros((blk, 1), F32)
            acc = jnp.zeros((blk, MOBA_HEAD_DIM), F32)
            for j in range(i + 1):
                p = jnp.exp((masked[j] - mx) * scale)
                den = den + jnp.sum(p, axis=-1, keepdims=True)
                acc = acc + _dot(p.astype(BF16), v[j * blk:(j + 1) * blk, :])
            outs.append(acc / den)
        o_ref[0, i * blk:(i + 1) * blk, :] = jnp.concatenate(outs, axis=1).astype(BF16)


def moba_prompt(qh, kh, vh):
    nb, nh, t, dh = qh.shape
    spec = pl.BlockSpec((1, HEADS_PER_STEP, t, dh), lambda b, g: (b, g, 0, 0))
    return pl.pallas_call(
        functools.partial(_moba_prompt_kernel, n_blk=t // MOBA_BLOCK),
        out_shape=jax.ShapeDtypeStruct((nb, t, nh * dh), BF16),
        grid=(nb, nh // HEADS_PER_STEP),
        in_specs=[spec, spec, spec],
        out_specs=pl.BlockSpec((1, t, HEADS_PER_STEP * dh), lambda b, g: (b, 0, g)),
        compiler_params=_params("parallel", "parallel"),
        name="moba_prompt",
    )(qh, kh, vh)


def _moba_sample_kernel(pt_ref, q_ref, kn_ref, vn_ref, *refs, n_pages, t_new):
    del pt_ref
    k_refs, v_refs, o_ref = refs[:n_pages], refs[n_pages:2 * n_pages], refs[2 * n_pages]
    rows = t_new * MOBA_HEADS
    scale = MOBA_HEAD_DIM ** -0.5
    q = q_ref[0]
    head_of_row = lax.broadcasted_iota(jnp.int32, (MOBA_HEADS, MOBA_WIDTH), 0)
    head_of_lane = lax.broadcasted_iota(jnp.int32, (MOBA_HEADS, MOBA_WIDTH), 1) // MOBA_HEAD_DIM
    own_head = head_of_row == head_of_lane
    q_bd = jnp.concatenate(
        [jnp.where(own_head, jnp.broadcast_to(q[t:t + 1, :], (MOBA_HEADS, MOBA_WIDTH)), 0.0) for t in range(t_new)],
        axis=0)
    q16 = q_bd.astype(BF16)
    pages = [_dot(q16, k_refs[j][...].reshape(MOBA_WIDTH, PAGE_SIZE).astype(BF16)) for j in range(n_pages)]

    pages_per_blk = MOBA_BLOCK // PAGE_SIZE
    n_past = n_pages // pages_per_blk
    gates = []
    for n in range(n_past):
        g = jnp.sum(pages[n * pages_per_blk], axis=-1, keepdims=True)
        for r in range(1, pages_per_blk):
            g = g + jnp.sum(pages[n * pages_per_blk + r], axis=-1, keepdims=True)
        gates.append(g)
    sel = _topk_select(gates)
    masked = []
    for j in range(n_pages):
        s_n = sel[j // pages_per_blk]
        masked.append(pages[j] if s_n is None else jnp.where(s_n, pages[j], -jnp.inf))
    tok_of_row = lax.broadcasted_iota(jnp.int32, (rows, 1), 0) // MOBA_HEADS
    kn = kn_ref[0]
    vn = vn_ref[0]
    own = []
    for j in range(t_new):
        s_j = jnp.sum(q_bd * kn[j:j + 1, :], axis=-1, keepdims=True)
        own.append(jnp.where(tok_of_row >= j, s_j, -jnp.inf))
    mx = own[0]
    for j in range(1, t_new):
        mx = jnp.maximum(mx, own[j])
    for j in range(n_pages):
        mx = jnp.maximum(mx, masked[j].max(axis=-1, keepdims=True))
    den = jnp.zeros((rows, 1), F32)
    acc = jnp.zeros((rows, MOBA_WIDTH), F32)
    for j in range(t_new):
        pj = jnp.exp((own[j] - mx) * scale)
        den = den + pj
        acc = acc + pj * vn[j:j + 1, :]
    for j in range(n_pages):
        pj = jnp.exp((masked[j] - mx) * scale)
        den = den + jnp.sum(pj, axis=-1, keepdims=True)
        acc = acc + _dot_nt(pj.astype(BF16), v_refs[j][...].reshape(MOBA_WIDTH, PAGE_SIZE).astype(BF16))
    acc = acc / den
    outs = []
    for t in range(t_new):
        a_t = acc[t * MOBA_HEADS:(t + 1) * MOBA_HEADS, :]
        outs.append(jnp.sum(jnp.where(own_head, a_t, 0.0), axis=0, keepdims=True))
    o_ref[0] = jnp.concatenate(outs, axis=0)


def moba_sample(page_table, q, k_new, v_new, cache_kt, cache_vt, layer):
    nb, t_new, _ = q.shape
    n_pages = page_table.shape[1]
    tok = pl.BlockSpec((1, t_new, MOBA_WIDTH), lambda b, pt: (b, 0, 0))

    def page(j):
        return pl.BlockSpec((None, None, MOBA_HEADS, MOBA_HEAD_DIM, PAGE_SIZE),
                            lambda b, pt: (pt[b * n_pages + j], layer, 0, 0, 0))

    pages = [page(j) for j in range(n_pages)]
    return pl.pallas_call(
        functools.partial(_moba_sample_kernel, n_pages=n_pages, t_new=t_new),
        out_shape=jax.ShapeDtypeStruct((nb, t_new, MOBA_WIDTH), F32),
        grid_spec=pltpu.PrefetchScalarGridSpec(
            num_scalar_prefetch=1,
            grid=(nb,),
            in_specs=[tok, tok, tok] + pages + pages,
            out_specs=tok,
        ),
        compiler_params=_params("parallel"),
        name="moba_sample",
    )(page_table.reshape(-1), q, k_new, v_new, *([cache_kt] * n_pages), *([cache_vt] * n_pages))


def _unit_lower_inverse(lows, c):
    row = lax.broadcasted_iota(jnp.int32, (c, c), 0)
    col = lax.broadcasted_iota(jnp.int32, (c, c), 1)
    eye = jnp.where(row == col, 1.0, 0.0)
    pair = row // 2 == col // 2
    xs = [eye - jnp.where(pair, low, 0.0) for low in lows]
    s = 2
    while s < c:
        sub = (row // (2 * s) == col // (2 * s)) & (row // s != col // s)
        xe = [_dot_hp(x, jnp.where(sub, low, 0.0)) for x, low in zip(xs, lows)]
        xs = [x - _dot_hp(t, x) for x, t in zip(xs, xe)]
        s *= 2
    return xs


def _gdn_kernel(qkv_ref, z_ref, ab_ref, cw_ref, alog_ref, dtb_ref, gn_ref, cbuf_ref, s0_ref,
                o_ref, cnew_ref, snew_ref, xx_ref, st_ref, *, bb, c, t_valid):
    ci = pl.program_id(1)

    @pl.when(ci == 0)
    def _():
        xx_ref[:, 0:8, :] = cbuf_ref[...]
        st_ref[...] = s0_ref[...]

    xx_ref[:, 8:8 + c, :] = qkv_ref[...]
    w = cw_ref[...]
    masked = t_valid < c
    valid = lax.broadcasted_iota(jnp.int32, (c, 1), 0) < t_valid
    row = lax.broadcasted_iota(jnp.int32, (c, c), 0)
    col = lax.broadcasted_iota(jnp.int32, (c, c), 1)
    tri = row >= col
    eye = row == col
    tri16 = jnp.where(tri, 1.0, 0.0).astype(BF16)
    gn = gn_ref[...]

    ys, gcs, betas = [], [], []
    for bi in range(bb):
        y = (w[0:1, :] * xx_ref[bi, 5:5 + c, :] + w[1:2, :] * xx_ref[bi, 6:6 + c, :]
             + w[2:3, :] * xx_ref[bi, 7:7 + c, :] + w[3:4, :] * xx_ref[bi, 8:8 + c, :])
        cnew_ref[bi] = xx_ref[bi, 5 + t_valid:8 + t_valid, :]
        xx_ref[bi, 0:8, :] = xx_ref[bi, c:c + 8, :]
        ys.append(_silu(y))
        ab = ab_ref[bi]
        g_all = -jnp.exp(alog_ref[...]) * _softplus(ab + dtb_ref[...])
        if masked:
            g_all = jnp.where(valid, g_all, 0.0)
        gcs.append(_dot_lhs_exact(tri16, g_all))
        betas.append(_sigmoid(ab))

    chains = [(bi, h) for bi in range(bb) for h in range(GDN_HEADS)]
    q_l, k_l, kb_l, vb_l, gcol_l, glast_l, decay_l = [], [], [], [], [], [], []
    for bi, h in chains:
        lo, hi = h * GDN_DK, (h + 1) * GDN_DK
        y = ys[bi]
        q = y[:, lo:hi]
        k = y[:, GDN_KEY_WIDTH + lo:GDN_KEY_WIDTH + hi]
        v = y[:, 2 * GDN_KEY_WIDTH + lo:2 * GDN_KEY_WIDTH + hi]
        q = q * lax.rsqrt(jnp.sum(q * q, axis=-1, keepdims=True) + EPS) * (GDN_DK ** -0.5)
        k = k * lax.rsqrt(jnp.sum(k * k, axis=-1, keepdims=True) + EPS)
        beta = betas[bi][:, GDN_HEADS + h:GDN_HEADS + h + 1]
        if masked:
            k = jnp.where(valid, k, 0.0)
            v = jnp.where(valid, v, 0.0)
            beta = jnp.where(valid, beta, 0.0)
        g_col = gcs[bi][:, h:h + 1]
        g_row = jnp.sum(jnp.where(eye, g_col, 0.0), axis=0, keepdims=True)
        q_l.append(q)
        k_l.append(k)
        kb_l.append(k * beta)
        vb_l.append(v * beta)
        gcol_l.append(g_col)
        glast_l.append(gcs[bi][c - 1:c, h:h + 1])
        decay_l.append(jnp.exp(jnp.where(tri, g_col - g_row, -jnp.inf)))
    k16_l = [k.astype(BF16) for k in k_l]
    low_l = [jnp.where(row > col, _dot_nt(kb.astype(BF16), k16) * decay, 0.0)
             for kb, k16, decay in zip(kb_l, k16_l, decay_l)]
    a_l = [(_dot_nt(q.astype(BF16), k16) * decay).astype(BF16) for q, k16, decay in zip(q_l, k16_l, decay_l)]
    t_l = [t.astype(BF16) for t in _unit_lower_inverse(low_l, c)]
    eg_l = [jnp.exp(g) for g in gcol_l]
    u_l = [_dot(t, vb.astype(BF16)) for t, vb in zip(t_l, vb_l)]
    wk_l = [_dot(t, (kb * eg).astype(BF16)).astype(BF16) for t, kb, eg in zip(t_l, kb_l, eg_l)]
    qd_l = [(q * eg).astype(BF16) for q, eg in zip(q_l, eg_l)]
    kd_l = [(k * jnp.exp(gl - g)).astype(BF16) for k, gl, g in zip(k_l, glast_l, gcol_l)]
    st_l = [st_ref[bi, h] for bi, h in chains]
    st16_l = [st.astype(BF16) for st in st_l]
    vn_l = [(u - _dot(wk, st16)).astype(BF16) for u, wk, st16 in zip(u_l, wk_l, st16_l)]
    o_l = [_dot(qd, st16) + _dot(a, vn) for qd, st16, a, vn in zip(qd_l, st16_l, a_l, vn_l)]
    sn_l = [st * jnp.exp(gl) + _dot_tn(kd, vn) for st, gl, kd, vn in zip(st_l, glast_l, kd_l, vn_l)]
    for (bi, h), o, sn in zip(chains, o_l, sn_l):
        lo, hi = h * GDN_DK, (h + 1) * GDN_DK
        st_ref[bi, h] = sn
        snew_ref[bi, h] = sn
        on = _rms(o, gn) * _silu(z_ref[bi, :, lo:hi])
        o_ref[bi, :, lo:hi] = on.astype(BF16)


def gdn(u3, ab3, conv_w, a_log, dt_bias, out_norm, conv_buf, s0, layer, bb, c, t_valid):
    nb, t, _ = u3.shape
    vec = lambda n: pl.BlockSpec((None, 1, n), lambda b, i: (layer, 0, 0))
    return pl.pallas_call(
        functools.partial(_gdn_kernel, bb=bb, c=c, t_valid=t_valid),
        out_shape=[jax.ShapeDtypeStruct((nb, t, GDN_WIDTH), BF16),
                   jax.ShapeDtypeStruct((nb, CONV_WIDTH - 1, GDN_CONV_DIM), F32),
                   jax.ShapeDtypeStruct((nb, GDN_HEADS, GDN_DK, GDN_DV), F32)],
        grid=(nb // bb, t // c),
        in_specs=[
            pl.BlockSpec((bb, c, GDN_CONV_DIM), lambda b, i: (b, i, U_GDN // GDN_CONV_DIM)),
            pl.BlockSpec((bb, c, GDN_WIDTH), lambda b, i: (b, i, U_Z // GDN_WIDTH)),
            pl.BlockSpec((bb, c, AB_COLS), lambda b, i: (b, i, 0)),
            pl.BlockSpec((None, CONV_WIDTH, GDN_CONV_DIM), lambda b, i: (layer, 0, 0)),
            vec(AB_COLS), vec(AB_COLS), vec(GDN_DV),
            pl.BlockSpec((bb, 8, GDN_CONV_DIM), lambda b, i: (b, 0, 0)),
            (pl.BlockSpec((bb, GDN_HEADS, GDN_DK, GDN_DV), lambda b, i: (b, 0, 0, 0)) if s0.ndim == 4 else
             pl.BlockSpec((bb, None, GDN_HEADS, GDN_DK, GDN_DV), lambda b, i: (b, layer, 0, 0, 0))),
        ],
        out_specs=[
            pl.BlockSpec((bb, c, GDN_WIDTH), lambda b, i: (b, i, 0)),
            pl.BlockSpec((bb, CONV_WIDTH - 1, GDN_CONV_DIM), lambda b, i: (b, 0, 0)),
            pl.BlockSpec((bb, GDN_HEADS, GDN_DK, GDN_DV), lambda b, i: (b, 0, 0, 0)),
        ],
        scratch_shapes=[pltpu.VMEM((bb, c + 8, GDN_CONV_DIM), F32),
                        pltpu.VMEM((bb, GDN_HEADS, GDN_DK, GDN_DV), F32)],
        compiler_params=_params("parallel", "arbitrary"),
        name="gdn",
    )(u3, u3, ab3, conv_w, a_log, dt_bias, out_norm, conv_buf, s0)


def _lru_gates(xf, wa, ba, wx, bx, lam):
    x16 = xf.astype(BF16)
    r = _sigmoid(_dot(x16, wa) + ba)
    i = _sigmoid(_dot(x16, wx) + bx)
    log_a = -LRU_C * r * _softplus(-lam)
    a = jnp.exp(log_a)
    b = jnp.sqrt(-_expm1(2.0 * log_a)) * (i * xf)
    return a, b


def _lru_prompt_kernel(x_ref, y_ref, cw_ref, cb_ref, wa_ref, ba_ref, wx_ref, bx_ref, lam_ref, cbuf_ref, h0_ref,
                       o_ref, cnew_ref, hlast_ref, xx_ref, a_s, b_s, h_s, hcar, *, nb, tc):
    ci = pl.program_id(0)

    @pl.when(ci == 0)
    def _():
        xx_ref[:, 0:8, :] = cbuf_ref[...]
        hcar[...] = h0_ref[...]

    xx_ref[:, 8:8 + tc, :] = x_ref[...]
    w = cw_ref[...]
    xc = (w[0:1, :] * xx_ref[:, 5:5 + tc, :] + w[1:2, :] * xx_ref[:, 6:6 + tc, :]
          + w[2:3, :] * xx_ref[:, 7:7 + tc, :] + w[3:4, :] * xx_ref[:, 8:8 + tc, :]) + cb_ref[...]
    cnew_ref[...] = xx_ref[:, tc + 5:tc + 8, :]
    xx_ref[:, 0:8, :] = xx_ref[:, tc:tc + 8, :]
    xf = xc.reshape(nb * tc, LRU_WIDTH)
    a, b = _lru_gates(xf, wa_ref[...], ba_ref[...], wx_ref[...], bx_ref[...], lam_ref[...])
    n_lane_tiles = LRU_WIDTH // LANES
    for j in range(n_lane_tiles):
        a_s[j] = a[:, j * LANES:(j + 1) * LANES]
        b_s[j] = b[:, j * LANES:(j + 1) * LANES]

    def step(t, hs):
        new = []
        for j in range(n_lane_tiles):
            h = a_s[j, pl.ds(t, nb, stride=tc), :] * hs[j] + b_s[j, pl.ds(t, nb, stride=tc), :]
            h_s[j, pl.ds(t, nb, stride=tc), :] = h
            new.append(h)
        return tuple(new)

    h0 = hcar[...]
    hs = lax.fori_loop(0, tc, step, tuple(h0[:, j * LANES:(j + 1) * LANES] for j in range(n_lane_tiles)))
    h = jnp.concatenate(hs, axis=1)
    hcar[...] = h
    hlast_ref[...] = h
    h_all = jnp.concatenate([h_s[j] for j in range(n_lane_tiles)], axis=1)
    out = h_all * jax.nn.gelu(y_ref[...].reshape(nb * tc, LRU_WIDTH))
    o_ref[...] = out.reshape(nb, tc, LRU_WIDTH).astype(BF16)


def lru_prompt(u3, conv_w, conv_b, wa, ba, wx, bx, lam, conv_buf, h0, layer, tc):
    nb, t, _ = u3.shape
    vec = pl.BlockSpec((None, 1, LRU_WIDTH), lambda i: (layer, 0, 0))
    mat = pl.BlockSpec((None, LRU_WIDTH, LRU_WIDTH), lambda i: (layer, 0, 0))
    return pl.pallas_call(
        functools.partial(_lru_prompt_kernel, nb=nb, tc=tc),
        out_shape=[jax.ShapeDtypeStruct((nb, t, LRU_WIDTH), BF16),
                   jax.ShapeDtypeStruct((nb, CONV_WIDTH - 1, LRU_WIDTH), F32),
                   jax.ShapeDtypeStruct((nb, LRU_WIDTH), F32)],
        grid=(t // tc,),
        in_specs=[
            pl.BlockSpec((nb, tc, LRU_WIDTH), lambda i: (0, i, U_X // LRU_WIDTH)),
            pl.BlockSpec((nb, tc, LRU_WIDTH), lambda i: (0, i, U_Y // LRU_WIDTH)),
            pl.BlockSpec((None, CONV_WIDTH, LRU_WIDTH), lambda i: (layer, 0, 0)),
            vec, mat, vec, mat, vec, vec,
            pl.BlockSpec((nb, 8, LRU_WIDTH), lambda i: (0, 0, 0)),
            pl.BlockSpec((nb, LRU_WIDTH), lambda i: (0, 0)),
        ],
        out_specs=[
            pl.BlockSpec((nb, tc, LRU_WIDTH), lambda i: (0, i, 0)),
            pl.BlockSpec((nb, CONV_WIDTH - 1, LRU_WIDTH), lambda i: (0, 0, 0)),
            pl.BlockSpec((nb, LRU_WIDTH), lambda i: (0, 0)),
        ],
        scratch_shapes=[pltpu.VMEM((nb, tc + 8, LRU_WIDTH), F32),
                        pltpu.VMEM((LRU_WIDTH // LANES, nb * tc, LANES), F32),
                        pltpu.VMEM((LRU_WIDTH // LANES, nb * tc, LANES), F32),
                        pltpu.VMEM((LRU_WIDTH // LANES, nb * tc, LANES), F32),
                        pltpu.VMEM((nb, LRU_WIDTH), F32)],
        compiler_params=_params("arbitrary"),
        name="lru_prompt",
    )(u3, u3, conv_w, conv_b, wa, ba, wx, bx, lam, conv_buf, h0)


def _lru_sample_kernel(x_ref, y_ref, cw_ref, cb_ref, wa_ref, ba_ref, wx_ref, bx_ref, lam_ref, cbuf_ref, h0_ref,
                       o_ref, cnew_ref, hlast_ref, *, t_new):
    w = cw_ref[...]
    xx = [cbuf_ref[j] for j in range(CONV_WIDTH - 1)] + [x_ref[j] for j in range(t_new)]
    for j in range(CONV_WIDTH - 1):
        cnew_ref[j] = xx[t_new + j]
    h = h0_ref[...]
    for t in range(t_new):
        xf = cb_ref[...] + w[0:1, :] * xx[t]
        for j in range(1, CONV_WIDTH):
            xf = xf + w[j:j + 1, :] * xx[t + j]
        a, b = _lru_gates(xf, wa_ref[...], ba_ref[...], wx_ref[...], bx_ref[...], lam_ref[...])
        h = a * h + b
        o_ref[t] = (h * jax.nn.gelu(y_ref[t])).astype(BF16)
    hlast_ref[...] = h


def lru_sample(x_tm, y_tm, conv_w, conv_b, wa, ba, wx, bx, lam, conv_buf_tm, h0, layer):
    t_new, nb, _ = x_tm.shape
    vec = pl.BlockSpec((None, 1, LRU_WIDTH), lambda i: (layer, 0, 0))
    mat = pl.BlockSpec((None, LRU_WIDTH, LRU_WIDTH), lambda i: (layer, 0, 0))
    tok = pl.BlockSpec((t_new, nb, LRU_WIDTH), lambda i: (0, 0, 0))
    buf = pl.BlockSpec((CONV_WIDTH - 1, nb, LRU_WIDTH), lambda i: (0, 0, 0))
    st = pl.BlockSpec((nb, LRU_WIDTH), lambda i: (0, 0))
    return pl.pallas_call(
        functools.partial(_lru_sample_kernel, t_new=t_new),
        out_shape=[jax.ShapeDtypeStruct((t_new, nb, LRU_WIDTH), BF16),
                   jax.ShapeDtypeStruct((CONV_WIDTH - 1, nb, LRU_WIDTH), F32),
                   jax.ShapeDtypeStruct((nb, LRU_WIDTH), F32)],
        grid=(1,),
        in_specs=[tok, tok, pl.BlockSpec((None, CONV_WIDTH, LRU_WIDTH), lambda i: (layer, 0, 0)),
                  vec, mat, vec, mat, vec, vec, buf, st],
        out_specs=[tok, buf, st],
        compiler_params=_params("arbitrary"),
        name="lru_sample",
    )(x_tm, y_tm, conv_w, conv_b, wa, ba, wx, bx, lam, conv_buf_tm, h0)


def _merge_kernel(oa_ref, od_ref, or_ref, ga_ref, gd_ref, gr_ref, x_ref, wa_ref, wd_ref, wr_ref, wo_ref, o_ref):
    merged = (_sigmoid(ga_ref[...]) * _dot(oa_ref[...], wa_ref[...])
              + _sigmoid(gd_ref[...]) * _dot(od_ref[...], wd_ref[...])
              + _sigmoid(gr_ref[...]) * _dot(or_ref[...], wr_ref[...]))
    o_ref[...] = x_ref[...] + _dot(merged.astype(BF16), wo_ref[...])


def merge(o_a, o_d, o_r, u, x, w_a, w_d, w_r, w_out, layer, tm):
    m, d = x.shape
    br = lambda n: pl.BlockSpec((tm, n), lambda i: (i, 0))
    gate = lambda j: pl.BlockSpec((tm, d), lambda i: (i, j))
    wspec = lambda k: pl.BlockSpec((None, k, d), lambda i: (layer, 0, 0))
    return pl.pallas_call(
        _merge_kernel,
        out_shape=jax.ShapeDtypeStruct((m, d), F32),
        grid=(m // tm,),
        in_specs=[br(MOBA_WIDTH), br(GDN_WIDTH), br(LRU_WIDTH), gate(0), gate(1), gate(2), br(d),
                  wspec(MOBA_WIDTH), wspec(GDN_WIDTH), wspec(LRU_WIDTH), wspec(d)],
        out_specs=br(d),
        compiler_params=_params("parallel"),
        name="merge",
    )(o_a, o_d, o_r, u, u, u, x, w_a, w_d, w_r, w_out)


def _ffn_kernel(x_ref, g_ref, wu_ref, wd_ref, o_ref, h_ref, acc_ref):
    f = pl.program_id(1)

    @pl.when(f == 0)
    def _():
        h_ref[...] = _rms(x_ref[...], g_ref[...]).astype(BF16)
        acc_ref[...] = jnp.zeros_like(acc_ref)

    a = jnp.maximum(_dot(h_ref[...], wu_ref[...]), 0.0)
    acc_ref[...] += _dot((a * a).astype(BF16), wd_ref[...])

    @pl.when(f == pl.num_programs(1) - 1)
    def _():
        o_ref[...] = x_ref[...] + acc_ref[...]


def ffn(x, gain, w_up, w_down, layer, tm, tf):
    m, d = x.shape
    f = w_up.shape[-1]
    return pl.pallas_call(
        _ffn_kernel,
        out_shape=jax.ShapeDtypeStruct((m, d), F32),
        grid=(m // tm, f // tf),
        in_specs=[
            pl.BlockSpec((tm, d), lambda i, j: (i, 0)),
            pl.BlockSpec((None, 1, d), lambda i, j: (layer, 0, 0)),
            pl.BlockSpec((None, d, tf), lambda i, j: (layer, 0, j)),
            pl.BlockSpec((None, tf, d), lambda i, j: (layer, j, 0)),
        ],
        out_specs=pl.BlockSpec((tm, d), lambda i, j: (i, 0)),
        scratch_shapes=[pltpu.VMEM((tm, d), BF16), pltpu.VMEM((tm, d), F32)],
        compiler_params=_params("parallel", "arbitrary"),
        name="ffn",
    )(x, gain, w_up, w_down)


def _ple_kernel(x_ref, g_ref, wg_ref, p_ref, wp_ref, o_ref):
    x = x_ref[...]
    gate = _sigmoid(_dot(_rms(x, g_ref[...]).astype(BF16), wg_ref[...]))
    o_ref[...] = x + gate * _dot(p_ref[...].astype(BF16), wp_ref[...])


def ple(x, gain, w_gate, p, w_proj, layer, tm):
    m, d = x.shape
    steps = m // tm
    return pl.pallas_call(
        _ple_kernel,
        out_shape=jax.ShapeDtypeStruct((m, d), F32),
        grid=(steps,),
        in_specs=[
            pl.BlockSpec((tm, d), lambda i: (i, 0)),
            pl.BlockSpec((None, 1, d), lambda i: (layer, 0, 0)),
            pl.BlockSpec((None, d, d), lambda i: (layer, 0, 0)),
            pl.BlockSpec((tm, PLE_DIM), lambda i: (layer * steps + i, 0)),
            pl.BlockSpec((None, PLE_DIM, d), lambda i: (layer, 0, 0)),
        ],
        out_specs=pl.BlockSpec((tm, d), lambda i: (i, 0)),
        compiler_params=_params("parallel"),
        name="ple",
    )(x, gain, w_gate, p, w_proj)


def _rope_tables(pos):
    half = MOBA_HEAD_DIM // 2
    inv_freq = ROPE_THETA ** (-jnp.arange(half, dtype=F32) / half)
    ang = pos.astype(F32)[:, None] * inv_freq[None, :]
    cos = jnp.cos(ang)
    sin = jnp.sin(ang)
    cos_h = jnp.concatenate([cos, cos], axis=-1)
    sin_h = jnp.concatenate([-sin, sin], axis=-1)
    return jnp.tile(cos_h, (1, MOBA_HEADS)), jnp.tile(sin_h, (1, MOBA_HEADS))


def _block_diag(w):
    l, h, n, _ = w.shape
    eye = jnp.eye(h, dtype=w.dtype)
    return (w[:, :, :, None, :] * eye[None, :, None, :, None]).reshape(l, h * n, h * n)


def _row3(v):
    return v[:, None, :]


def _pad_lanes(v, n):
    return jnp.pad(v, ((0, 0), (0, n - v.shape[-1])))[:, None, :]


def kernel(x_prompt, x_sample, cache_k, cache_v, state_gdn, state_gdn_conv, state_lru_h, state_lru_conv,
           page_table, p_prompt, p_sample, g_mix, w_in, moba_q_norm, moba_k_norm, w_branch_a, gdn_conv_w,
           gdn_a_log, gdn_dt_bias, gdn_out_norm, w_branch_d, lru_conv_w, lru_conv_b, lru_wa, lru_ba, lru_wx,
           lru_bx, lru_lambda, w_branch_r, w_out, g_ffn, w_up, w_down, g_ple, w_ple_gate, w_ple_proj):
    depth = w_in.shape[0]
    bp, seq, d = x_prompt.shape
    bs, t_new, _ = x_sample.shape
    n_pages = page_table.shape[1]
    past_len = n_pages * PAGE_SIZE
    mp, ms = bp * seq, bs * t_new

    o = 0
    offs = []
    for size in (3 * MOBA_WIDTH, GDN_CONV_DIM, GDN_WIDTH, GDN_HEADS, GDN_HEADS, LRU_WIDTH, LRU_WIDTH, 3 * D_MODEL):
        offs.append((o, o + size))
        o += size
    (m0, m1), (d0, d1), (z0, z1), (a0, a1), (b0, b1), (x0, x1), (y0, y1), (g0, g1) = offs
    w_main = jnp.concatenate([w_in[:, :, g0:g1], w_in[:, :, m0:m1], w_in[:, :, d0:d1], w_in[:, :, z0:z1],
                              w_in[:, :, x0:x1], w_in[:, :, y0:y1]], axis=-1).astype(BF16)
    w_ab = jnp.pad(w_in[:, :, a0:b1], ((0, 0), (0, 0), (0, AB_COLS - 2 * GDN_HEADS))).astype(BF16)
    w_a16, w_d16, w_r16, w_o16 = (w.astype(BF16) for w in (w_branch_a, w_branch_d, w_branch_r, w_out))
    w_up16, w_down16, w_pg16, w_pp16 = (w.astype(BF16) for w in (w_up, w_down, w_ple_gate, w_ple_proj))
    wa_bd = _block_diag(lru_wa).astype(BF16)
    wx_bd = _block_diag(lru_wx).astype(BF16)
    head_mean = jnp.kron(jnp.eye(MOBA_HEADS, dtype=F32),
                         jnp.full((MOBA_HEAD_DIM, MOBA_HEAD_DIM), 1.0 / MOBA_HEAD_DIM, F32)).astype(BF16)
    gq = _row3(jnp.tile(moba_q_norm, (1, MOBA_HEADS)))
    gk = _row3(jnp.tile(moba_k_norm, (1, MOBA_HEADS)))
    g_mix3, g_ffn3, g_ple3 = _row3(g_mix), _row3(g_ffn), _row3(g_ple)
    a_log3 = _pad_lanes(gdn_a_log, AB_COLS)
    dt_bias3 = _pad_lanes(gdn_dt_bias, AB_COLS)
    out_norm3 = _row3(gdn_out_norm)
    lru_cb3, lru_ba3, lru_bx3, lru_lam3 = _row3(lru_conv_b), _row3(lru_ba), _row3(lru_bx), _row3(lru_lambda)

    cos_p, sin_p = _rope_tables(jnp.arange(seq, dtype=jnp.int32))
    cos_s, sin_s = _rope_tables(past_len + jnp.arange(t_new, dtype=jnp.int32))
    cos_s, sin_s = jnp.tile(cos_s, (bs, 1)), jnp.tile(sin_s, (bs, 1))

    cache_kt = cache_k.transpose(0, 2, 3, 4, 1)
    cache_vt = cache_v.transpose(0, 2, 3, 4, 1)
    pp = p_prompt.reshape(depth * mp, PLE_DIM)
    ps = p_sample.reshape(depth * ms, PLE_DIM)

    zero_gconv = jnp.zeros((bp, 8, GDN_CONV_DIM), F32)
    zero_gstate = jnp.zeros((bp, GDN_HEADS, GDN_DK, GDN_DV), F32)
    zero_lconv = jnp.zeros((bp, 8, LRU_WIDTH), F32)
    zero_lh = jnp.zeros((bp, LRU_WIDTH), F32)
    c_s = 16

    xp = x_prompt.reshape(mp, d)
    xs = x_sample.reshape(ms, d)
    outs = {k: [] for k in ("kp", "vp", "ks", "vs", "gsp", "gss", "gcp", "gcs", "lhp", "lhs", "lcp", "lcs")}

    for l in range(depth):
        u, ab = norm_matmul(xp, g_mix3, w_main, w_ab, l, 1024, 1536)
        _, k_a, v_a, qh, kh, vh = moba_prep(u, cos_p, sin_p, head_mean, gq, gk, l, MOBA_BLOCK,
                                            seq // MOBA_BLOCK, True)
        o_a = moba_prompt(qh, kh, vh)
        o_a = o_a.reshape(mp, MOBA_WIDTH)
        u3 = u.reshape(bp, seq, U_COLS)
        o_d, gconv, gstate = gdn(u3, ab.reshape(bp, seq, AB_COLS), gdn_conv_w, a_log3, dt_bias3, out_norm3,
                                 zero_gconv, zero_gstate, l, 2, GDN_CHUNK, GDN_CHUNK)
        o_r, lconv, lh = lru_prompt(u3, lru_conv_w, lru_cb3, wa_bd, lru_ba3, wx_bd, lru_bx3, lru_lam3,
                                    zero_lconv, zero_lh, l, 256)
        xp = merge(o_a, o_d.reshape(mp, GDN_WIDTH), o_r.reshape(mp, LRU_WIDTH), u, xp,
                   w_a16, w_d16, w_r16, w_o16, l, 256)
        xp = ffn(xp, g_ffn3, w_up16, w_down16, l, 1024, 1024)
        xp = ple(xp, g_ple3, w_pg16, pp, w_pp16, l, 512)
        outs["kp"].append(k_a); outs["vp"].append(v_a); outs["gcp"].append(gconv); outs["gsp"].append(gstate)
        outs["lcp"].append(lconv); outs["lhp"].append(lh)

        u, ab = norm_matmul(xs, g_mix3, w_main, w_ab, l, ms, 1536)
        q_s, k_s, v_s = moba_prep(u, cos_s, sin_s, head_mean, gq, gk, l, ms, 1, False)
        tok_s = (bs, t_new, MOBA_WIDTH)
        o_a = moba_sample(page_table, q_s.reshape(tok_s), k_s.reshape(tok_s), v_s.reshape(tok_s),
                          cache_kt, cache_vt, l)
        o_a = o_a.reshape(ms, MOBA_WIDTH).astype(BF16)
        pad_t = ((0, 0), (0, c_s - t_new), (0, 0))
        u3 = jnp.pad(u.reshape(bs, t_new, U_COLS), pad_t)
        ab3 = jnp.pad(ab.reshape(bs, t_new, AB_COLS), pad_t)
        gbuf = jnp.pad(state_gdn_conv[:, l], ((0, 0), (8 - (CONV_WIDTH - 1), 0), (0, 0)))
        o_d, gconv, gstate = gdn(u3, ab3, gdn_conv_w, a_log3, dt_bias3, out_norm3,
                                 gbuf, state_gdn, l, 8, c_s, t_new)
        o_d = o_d[:, :t_new].reshape(ms, GDN_WIDTH)
        us = u.reshape(bs, t_new, U_COLS)
        x_tm = us[:, :, U_X:U_X + LRU_WIDTH].transpose(1, 0, 2)
        y_tm = us[:, :, U_Y:U_Y + LRU_WIDTH].transpose(1, 0, 2)
        o_r, lconv, lh = lru_sample(x_tm, y_tm, lru_conv_w, lru_cb3, wa_bd, lru_ba3, wx_bd, lru_bx3, lru_lam3,
                                    state_lru_conv[:, l].transpose(1, 0, 2), state_lru_h[:, l], l)
        o_r = o_r.transpose(1, 0, 2).reshape(ms, LRU_WIDTH)
        xs = merge(o_a, o_d, o_r, u, xs, w_a16, w_d16, w_r16, w_o16, l, 256)
        xs = ffn(xs, g_ffn3, w_up16, w_down16, l, ms, 1024)
        xs = ple(xs, g_ple3, w_pg16, ps, w_pp16, l, ms)
        outs["ks"].append(k_s); outs["vs"].append(v_s); outs["gcs"].append(gconv); outs["gss"].append(gstate)
        outs["lcs"].append(lconv.transpose(1, 0, 2)); outs["lhs"].append(lh)

    hd = (MOBA_HEADS, MOBA_HEAD_DIM)
    k_prompt = jnp.stack(outs["kp"], axis=1).reshape(bp, seq, depth, *hd)
    v_prompt = jnp.stack(outs["vp"], axis=1).reshape(bp, seq, depth, *hd)
    k_sample = jnp.stack(outs["ks"], axis=1).reshape(bs, t_new, depth, *hd)
    v_sample = jnp.stack(outs["vs"], axis=1).reshape(bs, t_new, depth, *hd)
    return (xp.reshape(bp, seq, d), xs.reshape(bs, t_new, d), k_prompt, v_prompt, k_sample, v_sample,
            jnp.stack(outs["gsp"], axis=1), jnp.stack(outs["gss"], axis=1),
            jnp.stack(outs["gcp"], axis=1), jnp.stack(outs["gcs"], axis=1),
            jnp.stack(outs["lhp"], axis=1), jnp.stack(outs["lhs"], axis=1),
            jnp.stack(outs["lcp"], axis=1), jnp.stack(outs["lcs"], axis=1))
```

```python
import functools
import math

import jax
import jax.numpy as jnp
from jax import lax
from jax.experimental import pallas as pl
from jax.experimental.pallas import tpu as pltpu

F32 = jnp.float32
BF16 = jnp.bfloat16

D_MODEL = 1024
MOBA_HEADS = 8
MOBA_HEAD_DIM = 64
MOBA_WIDTH = MOBA_HEADS * MOBA_HEAD_DIM
MOBA_BLOCK = 256
MOBA_TOPK = 3
ROPE_THETA = 10000.0
PAGE_SIZE = 128
GDN_HEADS = 4
GDN_DK = 128
GDN_DV = 128
GDN_KEY_WIDTH = GDN_HEADS * GDN_DK
GDN_WIDTH = GDN_HEADS * GDN_DV
GDN_CONV_DIM = 2 * GDN_KEY_WIDTH + GDN_WIDTH
GDN_CHUNK = 64
CONV_WIDTH = 4
LRU_WIDTH = 512
LRU_HEADS = 8
LRU_BLOCK = LRU_WIDTH // LRU_HEADS
LRU_C = 8.0
D_FF = 4 * D_MODEL
PLE_DIM = 256
EPS = 1e-6

U_GATES = 0
U_MOBA = 3 * D_MODEL
U_GDN = U_MOBA + 3 * MOBA_WIDTH
U_Z = U_GDN + GDN_CONV_DIM
U_X = U_Z + GDN_WIDTH
U_Y = U_X + LRU_WIDTH
U_COLS = U_Y + LRU_WIDTH
AB_COLS = 128

VMEM_LIMIT = 56 * 1024 * 1024
LANES = 128
LOG2E = math.log2(math.e)
LRU_ROW_PAD = 8


def _params(*sem):
    return pltpu.CompilerParams(dimension_semantics=sem, vmem_limit_bytes=VMEM_LIMIT)


def _dot(a, b):
    return jnp.dot(a, b, preferred_element_type=F32)


def _dot_nt(a, b):
    return lax.dot_general(a, b, (((1,), (1,)), ((), ())), preferred_element_type=F32)


def _dot_tn(a, b):
    return lax.dot_general(a, b, (((0,), (0,)), ((), ())), preferred_element_type=F32)


def _split2(x):
    hi = x.astype(BF16)
    lo = (x - hi.astype(F32)).astype(BF16)
    return hi, lo


def _split3(x):
    hi = x.astype(BF16)
    r = x - hi.astype(F32)
    mid = r.astype(BF16)
    lo = (r - mid.astype(F32)).astype(BF16)
    return hi, mid, lo


def _dot_hp(a, b):
    ah, al = _split2(a)
    bh, bl = _split2(b)
    return _dot(ah, bh) + (_dot(ah, bl) + _dot(al, bh))


def _dot_lhs_exact(m_bf16, x):
    h, m, l = _split3(x)
    return _dot(m_bf16, h) + (_dot(m_bf16, m) + _dot(m_bf16, l))


def _dot_rhs_exact(x, m_bf16):
    h, m, l = _split3(x)
    return _dot(h, m_bf16) + (_dot(m, m_bf16) + _dot(l, m_bf16))


def _rms(xf, gain):
    ms = jnp.mean(xf * xf, axis=-1, keepdims=True)
    return xf * lax.rsqrt(ms + EPS) * gain


def _sigmoid(x):
    return 1.0 / (1.0 + jnp.exp(-x))


def _silu(x):
    return x * _sigmoid(x)


def _softplus(x):
    return jnp.maximum(x, 0.0) + jnp.log1p(jnp.exp(-jnp.abs(x)))


def _expm1(x):
    u = jnp.exp(x)
    um1 = u - 1.0
    lu = jnp.log(u)
    near = um1 * x / jnp.where(lu == 0.0, 1.0, lu)
    near = jnp.where(um1 == 0.0, x, near)
    return jnp.where(jnp.abs(x) < 0.5, near, um1)


def _norm_matmul_kernel(x_ref, g_ref, w_ref, w2_ref, o_ref, o2_ref, h_ref):
    @pl.when(pl.program_id(1) == 0)
    def _():
        h = _rms(x_ref[...], g_ref[...]).astype(BF16)
        h_ref[...] = h
        o2_ref[...] = _dot(h, w2_ref[...])

    o_ref[...] = _dot(h_ref[...], w_ref[...])


def norm_matmul(x, gain, w, w2, layer, tm, tn):
    m, d = x.shape
    n, n2 = w.shape[-1], w2.shape[-1]
    return pl.pallas_call(
        _norm_matmul_kernel,
        out_shape=[jax.ShapeDtypeStruct((m, n), F32), jax.ShapeDtypeStruct((m, n2), F32)],
        grid=(m // tm, n // tn),
        in_specs=[
            pl.BlockSpec((tm, d), lambda i, j: (i, 0)),
            pl.BlockSpec((None, 1, d), lambda i, j: (layer, 0, 0)),
            pl.BlockSpec((None, d, tn), lambda i, j: (layer, 0, j)),
            pl.BlockSpec((None, d, n2), lambda i, j: (layer, 0, 0)),
        ],
        out_specs=[pl.BlockSpec((tm, tn), lambda i, j: (i, j)),
                   pl.BlockSpec((tm, n2), lambda i, j: (i, 0))],
        scratch_shapes=[pltpu.VMEM((tm, d), BF16)],
        compiler_params=_params("parallel", "arbitrary"),
        name="norm_matmul",
    )(x, gain, w, w2)


def _moba_prep_kernel(qkv_ref, cos_ref, sin_ref, bd_ref, gq_ref, gk_ref, q_ref, k_ref, v_ref, *hm_refs):
    cos = cos_ref[...]
    sin = sin_ref[...]
    bd = bd_ref[...]
    lane = lax.broadcasted_iota(jnp.int32, cos.shape, 1)
    first_half = (lane % MOBA_HEAD_DIM) < (MOBA_HEAD_DIM // 2)

    def norm_rot(x, gain):
        ms = _dot_rhs_exact(x * x, bd)
        y = x * lax.rsqrt(ms + EPS) * gain
        partner = jnp.where(first_half,
                            pltpu.roll(y, MOBA_WIDTH - MOBA_HEAD_DIM // 2, 1),
                            pltpu.roll(y, MOBA_HEAD_DIM // 2, 1))
        return y * cos + partner * sin

    q = norm_rot(qkv_ref[:, 0:MOBA_WIDTH], gq_ref[...])
    k = norm_rot(qkv_ref[:, MOBA_WIDTH:2 * MOBA_WIDTH], gk_ref[...])
    v = qkv_ref[:, 2 * MOBA_WIDTH:3 * MOBA_WIDTH]
    q_ref[...] = q
    k_ref[...] = k
    v_ref[...] = v
    if hm_refs:
        kh_ref, = hm_refs
        for h in range(MOBA_HEADS):
            kh_ref[0, h] = k[:, h * MOBA_HEAD_DIM:(h + 1) * MOBA_HEAD_DIM].astype(BF16)


def moba_prep(u, cos, sin, bd, gq, gk, layer, tq, seq_blocks, head_major):
    m = u.shape[0]
    n_steps = m // tq
    out_shape = [jax.ShapeDtypeStruct((m, MOBA_WIDTH), F32)] * 3
    out_specs = [pl.BlockSpec((tq, MOBA_WIDTH), lambda i: (i, 0))] * 3
    if head_major:
        nb = n_steps // seq_blocks
        out_shape.append(jax.ShapeDtypeStruct((nb, MOBA_HEADS, seq_blocks * tq, MOBA_HEAD_DIM), BF16))
        out_specs.append(pl.BlockSpec((1, MOBA_HEADS, tq, MOBA_HEAD_DIM),
                                      lambda i: (i // seq_blocks, 0, i % seq_blocks, 0)))
    tab = pl.BlockSpec((tq, MOBA_WIDTH), lambda i: (i % seq_blocks, 0))
    vec = pl.BlockSpec((None, 1, MOBA_WIDTH), lambda i: (layer, 0, 0))
    return pl.pallas_call(
        _moba_prep_kernel,
        out_shape=out_shape,
        grid=(n_steps,),
        in_specs=[
            pl.BlockSpec((tq, 3 * MOBA_WIDTH), lambda i: (i, U_MOBA // (3 * MOBA_WIDTH))),
            tab, tab,
            pl.BlockSpec((MOBA_WIDTH, MOBA_WIDTH), lambda i: (0, 0)),
            vec, vec,
        ],
        out_specs=out_specs,
        compiler_params=_params("parallel"),
        name="moba_prep",
    )(u, cos, sin, bd, gq, gk)


def _topk_select(gates):
    n = len(gates)
    if n <= MOBA_TOPK:
        return [None] * n
    sel = []
    for a in range(n):
        rank = jnp.zeros(gates[a].shape, F32)
        for b in range(n):
            if b == a:
                continue
            beats = (gates[b] >= gates[a]) if b < a else (gates[b] > gates[a])
            rank = rank + jnp.where(beats, 1.0, 0.0)
        sel.append(rank < float(MOBA_TOPK))
    return sel


HEADS_PER_STEP = LANES // MOBA_HEAD_DIM


def _moba_prompt_kernel(q_ref, k_ref, v_ref, o_ref, *, n_blk):
    blk = MOBA_BLOCK
    dh = MOBA_HEAD_DIM
    scale = dh ** -0.5
    key = lax.broadcasted_iota(jnp.int32, (blk, blk), 0)
    qry = lax.broadcasted_iota(jnp.int32, (blk, blk), 1)
    causal = key <= qry
    vt = [v_ref[0, j * blk:(j + 1) * blk, :].T.astype(BF16) for j in range(n_blk)]
    def scores(i, hh):
        qt = q_ref[0, i * blk:(i + 1) * blk, :].T.astype(BF16)
        return _dot(k_ref[0, hh, 0:(i + 1) * blk, :], qt[hh * dh:(hh + 1) * dh, :])

    def attend(i, hh, s):
        parts = [s[j * blk:(j + 1) * blk, :] for j in range(i + 1)]
        gates = [jnp.sum(parts[j], axis=0, keepdims=True) for j in range(i)]
        sel = _topk_select(gates)
        masked = []
        for j in range(i):
            masked.append(parts[j] if sel[j] is None else jnp.where(sel[j], parts[j], -jnp.inf))
        masked.append(jnp.where(causal, parts[i], -jnp.inf))
        mx = masked[0].max(axis=0, keepdims=True)
        for j in range(1, i + 1):
            mx = jnp.maximum(mx, masked[j].max(axis=0, keepdims=True))
        den = jnp.zeros((1, blk), F32)
        acc = jnp.zeros((dh, blk), F32)
        for j in range(i + 1):
            p = jnp.exp2((masked[j] - mx) * (scale * LOG2E))
            den = den + jnp.sum(p, axis=0, keepdims=True)
            acc = acc + _dot(vt[j][hh * dh:(hh + 1) * dh, :], p.astype(BF16))
        return acc / den

    units = [(i, hh) for i in range(n_blk) for hh in range(HEADS_PER_STEP)]
    s_next = scores(*units[0])
    outs = []
    for n, (i, hh) in enumerate(units):
        s_cur = s_next
        if n + 1 < len(units):
            s_next = scores(*units[n + 1])
        outs.append(attend(i, hh, s_cur))
        if hh == HEADS_PER_STEP - 1:
            o_ref[0, i * blk:(i + 1) * blk, :] = jnp.concatenate(outs, axis=0).T.astype(BF16)
            outs = []


def moba_prompt(q, kh, v):
    nb, nh, t, dh = kh.shape
    tok = pl.BlockSpec((1, t, HEADS_PER_STEP * dh), lambda b, g: (b, 0, g))
    return pl.pallas_call(
        functools.partial(_moba_prompt_kernel, n_blk=t // MOBA_BLOCK),
        out_shape=jax.ShapeDtypeStruct((nb, t, nh * dh), BF16),
        grid=(nb, nh // HEADS_PER_STEP),
        in_specs=[tok, pl.BlockSpec((1, HEADS_PER_STEP, t, dh), lambda b, g: (b, g, 0, 0)), tok],
        out_specs=tok,
        compiler_params=_params("parallel", "parallel"),
        name="moba_prompt",
    )(q, kh, v)


def _moba_sample_kernel(pt_ref, q_ref, kn_ref, vn_ref, *refs, n_pages, t_new):
    del pt_ref
    k_refs, v_refs, o_ref = refs[:n_pages], refs[n_pages:2 * n_pages], refs[2 * n_pages]
    rows = t_new * MOBA_HEADS
    scale = MOBA_HEAD_DIM ** -0.5
    q = q_ref[0]
    head_of_row = lax.broadcasted_iota(jnp.int32, (MOBA_HEADS, MOBA_WIDTH), 0)
    head_of_lane = lax.broadcasted_iota(jnp.int32, (MOBA_HEADS, MOBA_WIDTH), 1) // MOBA_HEAD_DIM
    own_head = head_of_row == head_of_lane
    q_bd = jnp.concatenate(
        [jnp.where(own_head, jnp.broadcast_to(q[t:t + 1, :], (MOBA_HEADS, MOBA_WIDTH)), 0.0) for t in range(t_new)],
        axis=0)
    q16 = q_bd.astype(BF16)
    pages = [_dot(q16, k_refs[j][...].reshape(MOBA_WIDTH, PAGE_SIZE).astype(BF16)) for j in range(n_pages)]

    pages_per_blk = MOBA_BLOCK // PAGE_SIZE
    n_past = n_pages // pages_per_blk
    gates = []
    for n in range(n_past):
        g = jnp.sum(pages[n * pages_per_blk], axis=-1, keepdims=True)
        for r in range(1, pages_per_blk):
            g = g + jnp.sum(pages[n * pages_per_blk + r], axis=-1, keepdims=True)
        gates.append(g)
    sel = _topk_select(gates)
    masked = []
    for j in range(n_pages):
        s_n = sel[j // pages_per_blk]
        masked.append(pages[j] if s_n is None else jnp.where(s_n, pages[j], -jnp.inf))
    tok_of_row = lax.broadcasted_iota(jnp.int32, (rows, 1), 0) // MOBA_HEADS
    kn = kn_ref[0]
    vn = vn_ref[0]
    own = []
    for j in range(t_new):
        s_j = jnp.sum(q_bd * kn[j:j + 1, :], axis=-1, keepdims=True)
        own.append(jnp.where(tok_of_row >= j, s_j, -jnp.inf))
    mx = own[0]
    for j in range(1, t_new):
        mx = jnp.maximum(mx, own[j])
    for j in range(n_pages):
        mx = jnp.maximum(mx, masked[j].max(axis=-1, keepdims=True))
    den = jnp.zeros((rows, 1), F32)
    acc = jnp.zeros((rows, MOBA_WIDTH), F32)
    for j in range(t_new):
        pj = jnp.exp((own[j] - mx) * scale)
        den = den + pj
        acc = acc + pj * vn[j:j + 1, :]
    for j in range(n_pages):
        pj = jnp.exp((masked[j] - mx) * scale)
        den = den + jnp.sum(pj, axis=-1, keepdims=True)
        acc = acc + _dot_nt(pj.astype(BF16), v_refs[j][...].reshape(MOBA_WIDTH, PAGE_SIZE).astype(BF16))
    acc = acc / den
    outs = []
    for t in range(t_new):
        a_t = acc[t * MOBA_HEADS:(t + 1) * MOBA_HEADS, :]
        outs.append(jnp.sum(jnp.where(own_head, a_t, 0.0), axis=0, keepdims=True))
    o_ref[0] = jnp.concatenate(outs, axis=0)


def moba_sample(page_table, q, k_new, v_new, cache_kt, cache_vt, layer):
    nb, t_new, _ = q.shape
    n_pages = page_table.shape[1]
    tok = pl.BlockSpec((1, t_new, MOBA_WIDTH), lambda b, pt: (b, 0, 0))

    def page(j):
        return pl.BlockSpec((None, None, MOBA_HEADS, MOBA_HEAD_DIM, PAGE_SIZE),
                            lambda b, pt: (pt[b * n_pages + j], layer, 0, 0, 0))

    pages = [page(j) for j in range(n_pages)]
    return pl.pallas_call(
        functools.partial(_moba_sample_kernel, n_pages=n_pages, t_new=t_new),
        out_shape=jax.ShapeDtypeStruct((nb, t_new, MOBA_WIDTH), F32),
        grid_spec=pltpu.PrefetchScalarGridSpec(
            num_scalar_prefetch=1,
            grid=(nb,),
            in_specs=[tok, tok, tok] + pages + pages,
            out_specs=tok,
        ),
        compiler_params=_params("parallel"),
        name="moba_sample",
    )(page_table.reshape(-1), q, k_new, v_new, *([cache_kt] * n_pages), *([cache_vt] * n_pages))


def _unit_lower_inverse(lows, c):
    row = lax.broadcasted_iota(jnp.int32, (c, c), 0)
    col = lax.broadcasted_iota(jnp.int32, (c, c), 1)
    eye = jnp.where(row == col, 1.0, 0.0)
    pair = row // 2 == col // 2
    xs = [eye - jnp.where(pair, low, 0.0) for low in lows]
    s = 2
    while s < c:
        sub = (row // (2 * s) == col // (2 * s)) & (row // s != col // s)
        xe = [_dot_hp(x, jnp.where(sub, low, 0.0)) for x, low in zip(xs, lows)]
        xs = [x - _dot_hp(t, x) for x, t in zip(xs, xe)]
        s *= 2
    return xs


def _gdn_kernel(qkv_ref, z_ref, ab_ref, cw_ref, alog_ref, dtb_ref, gn_ref, cbuf_ref, s0_ref,
                o_ref, cnew_ref, snew_ref, xx_ref, st_ref, *, bb, c, t_valid):
    ci = pl.program_id(1)

    @pl.when(ci == 0)
    def _():
        xx_ref[:, 0:8, :] = cbuf_ref[...]
        st_ref[...] = s0_ref[...]

    xx_ref[:, 8:8 + c, :] = qkv_ref[...]
    w = cw_ref[...]
    masked = t_valid < c
    valid = lax.broadcasted_iota(jnp.int32, (c, 1), 0) < t_valid
    row = lax.broadcasted_iota(jnp.int32, (c, c), 0)
    col = lax.broadcasted_iota(jnp.int32, (c, c), 1)
    tri = row >= col
    eye = row == col
    tri16 = jnp.where(tri, 1.0, 0.0).astype(BF16)
    gn = gn_ref[...]

    ys, gcs, betas = [], [], []
    for bi in range(bb):
        y = (w[0:1, :] * xx_ref[bi, 5:5 + c, :] + w[1:2, :] * xx_ref[bi, 6:6 + c, :]
             + w[2:3, :] * xx_ref[bi, 7:7 + c, :] + w[3:4, :] * xx_ref[bi, 8:8 + c, :])
        cnew_ref[bi] = xx_ref[bi, 5 + t_valid:8 + t_valid, :]
        xx_ref[bi, 0:8, :] = xx_ref[bi, c:c + 8, :]
        ys.append(_silu(y))
        ab = ab_ref[bi]
        g_all = -jnp.exp(alog_ref[...]) * _softplus(ab + dtb_ref[...])
        if masked:
            g_all = jnp.where(valid, g_all, 0.0)
        gcs.append(_dot_lhs_exact(tri16, g_all))
        betas.append(_sigmoid(ab))

    chains = [(bi, h) for bi in range(bb) for h in range(GDN_HEADS)]
    q_l, k_l, kb_l, vb_l, gcol_l, glast_l, decay_l = [], [], [], [], [], [], []
    for bi, h in chains:
        lo, hi = h * GDN_DK, (h + 1) * GDN_DK
        y = ys[bi]
        q = y[:, lo:hi]
        k = y[:, GDN_KEY_WIDTH + lo:GDN_KEY_WIDTH + hi]
        v = y[:, 2 * GDN_KEY_WIDTH + lo:2 * GDN_KEY_WIDTH + hi]
        q = q * lax.rsqrt(jnp.sum(q * q, axis=-1, keepdims=True) + EPS) * (GDN_DK ** -0.5)
        k = k * lax.rsqrt(jnp.sum(k * k, axis=-1, keepdims=True) + EPS)
        beta = betas[bi][:, GDN_HEADS + h:GDN_HEADS + h + 1]
        if masked:
            k = jnp.where(valid, k, 0.0)
            v = jnp.where(valid, v, 0.0)
            beta = jnp.where(valid, beta, 0.0)
        g_col = gcs[bi][:, h:h + 1]
        g_row = jnp.sum(jnp.where(eye, g_col, 0.0), axis=0, keepdims=True)
        q_l.append(q)
        k_l.append(k)
        kb_l.append(k * beta)
        vb_l.append(v * beta)
        gcol_l.append(g_col)
        glast_l.append(gcs[bi][c - 1:c, h:h + 1])
        decay_l.append(jnp.exp(jnp.where(tri, g_col - g_row, -jnp.inf)))
    k16_l = [k.astype(BF16) for k in k_l]
    low_l = [jnp.where(row > col, _dot_nt(kb.astype(BF16), k16) * decay, 0.0)
             for kb, k16, decay in zip(kb_l, k16_l, decay_l)]
    a_l = [(_dot_nt(q.astype(BF16), k16) * decay).astype(BF16) for q, k16, decay in zip(q_l, k16_l, decay_l)]
    t_l = [t.astype(BF16) for t in _unit_lower_inverse(low_l, c)]
    eg_l = [jnp.exp(g) for g in gcol_l]
    u_l = [_dot(t, vb.astype(BF16)) for t, vb in zip(t_l, vb_l)]
    wk_l = [_dot(t, (kb * eg).astype(BF16)).astype(BF16) for t, kb, eg in zip(t_l, kb_l, eg_l)]
    qd_l = [(q * eg).astype(BF16) for q, eg in zip(q_l, eg_l)]
    kd_l = [(k * jnp.exp(gl - g)).astype(BF16) for k, gl, g in zip(k_l, glast_l, gcol_l)]
    st_l = [st_ref[bi, h] for bi, h in chains]
    st16_l = [st.astype(BF16) for st in st_l]
    vn_l = [(u - _dot(wk, st16)).astype(BF16) for u, wk, st16 in zip(u_l, wk_l, st16_l)]
    o_l = [_dot(qd, st16) + _dot(a, vn) for qd, st16, a, vn in zip(qd_l, st16_l, a_l, vn_l)]
    sn_l = [st * jnp.exp(gl) + _dot_tn(kd, vn) for st, gl, kd, vn in zip(st_l, glast_l, kd_l, vn_l)]
    for (bi, h), o, sn in zip(chains, o_l, sn_l):
        lo, hi = h * GDN_DK, (h + 1) * GDN_DK
        st_ref[bi, h] = sn
        snew_ref[bi, h] = sn
        on = _rms(o, gn) * _silu(z_ref[bi, :, lo:hi])
        o_ref[bi, :, lo:hi] = on.astype(BF16)


def gdn(u3, ab3, conv_w, a_log, dt_bias, out_norm, conv_buf, s0, layer, bb, c, t_valid):
    nb, t, _ = u3.shape
    vec = lambda n: pl.BlockSpec((None, 1, n), lambda b, i: (layer, 0, 0))
    return pl.pallas_call(
        functools.partial(_gdn_kernel, bb=bb, c=c, t_valid=t_valid),
        out_shape=[jax.ShapeDtypeStruct((nb, t, GDN_WIDTH), BF16),
                   jax.ShapeDtypeStruct((nb, CONV_WIDTH - 1, GDN_CONV_DIM), F32),
                   jax.ShapeDtypeStruct((nb, GDN_HEADS, GDN_DK, GDN_DV), F32)],
        grid=(nb // bb, t // c),
        in_specs=[
            pl.BlockSpec((bb, c, GDN_CONV_DIM), lambda b, i: (b, i, U_GDN // GDN_CONV_DIM)),
            pl.BlockSpec((bb, c, GDN_WIDTH), lambda b, i: (b, i, U_Z // GDN_WIDTH)),
            pl.BlockSpec((bb, c, AB_COLS), lambda b, i: (b, i, 0)),
            pl.BlockSpec((None, CONV_WIDTH, GDN_CONV_DIM), lambda b, i: (layer, 0, 0)),
            vec(AB_COLS), vec(AB_COLS), vec(GDN_DV),
            pl.BlockSpec((bb, 8, GDN_CONV_DIM), lambda b, i: (b, 0, 0)),
            (pl.BlockSpec((bb, GDN_HEADS, GDN_DK, GDN_DV), lambda b, i: (b, 0, 0, 0)) if s0.ndim == 4 else
             pl.BlockSpec((bb, None, GDN_HEADS, GDN_DK, GDN_DV), lambda b, i: (b, layer, 0, 0, 0))),
        ],
        out_specs=[
            pl.BlockSpec((bb, c, GDN_WIDTH), lambda b, i: (b, i, 0)),
            pl.BlockSpec((bb, CONV_WIDTH - 1, GDN_CONV_DIM), lambda b, i: (b, 0, 0)),
            pl.BlockSpec((bb, GDN_HEADS, GDN_DK, GDN_DV), lambda b, i: (b, 0, 0, 0)),
        ],
        scratch_shapes=[pltpu.VMEM((bb, c + 8, GDN_CONV_DIM), F32),
                        pltpu.VMEM((bb, GDN_HEADS, GDN_DK, GDN_DV), F32)],
        compiler_params=_params("parallel", "arbitrary"),
        name="gdn",
    )(u3, u3, ab3, conv_w, a_log, dt_bias, out_norm, conv_buf, s0)


def _lru_gates(xf, wa, ba, wx, bx, lam):
    x16 = xf.astype(BF16)
    r = _sigmoid(_dot(x16, wa) + ba)
    i = _sigmoid(_dot(x16, wx) + bx)
    log_a = -LRU_C * r * _softplus(-lam)
    a = jnp.exp(log_a)
    b = jnp.sqrt(-_expm1(2.0 * log_a)) * (i * xf)
    return a, b


def _lru_prompt_kernel(x_ref, y_ref, cw_ref, cb_ref, wa_ref, ba_ref, wx_ref, bx_ref, lam_ref, cbuf_ref, h0_ref,
                       o_ref, cnew_ref, hlast_ref, xx_ref, a_s, b_s, h_s, hcar, *, nb, tc):
    ci = pl.program_id(0)

    @pl.when(ci == 0)
    def _():
        xx_ref[:, 0:8, :] = cbuf_ref[...]
        hcar[...] = h0_ref[...]

    xx_ref[:, 8:8 + tc, :] = x_ref[...]
    w = cw_ref[...]
    xc = (w[0:1, :] * xx_ref[:, 5:5 + tc, :] + w[1:2, :] * xx_ref[:, 6:6 + tc, :]
          + w[2:3, :] * xx_ref[:, 7:7 + tc, :] + w[3:4, :] * xx_ref[:, 8:8 + tc, :]) + cb_ref[...]
    cnew_ref[...] = xx_ref[:, tc + 5:tc + 8, :]
    xx_ref[:, 0:8, :] = xx_ref[:, tc:tc + 8, :]
    xf = xc.reshape(nb * tc, LRU_WIDTH)
    a, b = _lru_gates(xf, wa_ref[...], ba_ref[...], wx_ref[...], bx_ref[...], lam_ref[...])
    n_lane_tiles = LRU_WIDTH // LANES
    pitch = tc + LRU_ROW_PAD
    for j in range(n_lane_tiles):
        for bi in range(nb):
            a_s[j, bi * pitch:bi * pitch + tc, :] = a[bi * tc:(bi + 1) * tc, j * LANES:(j + 1) * LANES]
            b_s[j, bi * pitch:bi * pitch + tc, :] = b[bi * tc:(bi + 1) * tc, j * LANES:(j + 1) * LANES]

    def step(t, hs):
        new = []
        for j in range(n_lane_tiles):
            h = a_s[j, pl.ds(t, nb, stride=pitch), :] * hs[j] + b_s[j, pl.ds(t, nb, stride=pitch), :]
            h_s[j, pl.ds(t, nb, stride=pitch), :] = h
            new.append(h)
        return tuple(new)

    h0 = hcar[...]
    hs = lax.fori_loop(0, tc, step, tuple(h0[:, j * LANES:(j + 1) * LANES] for j in range(n_lane_tiles)),
                       unroll=8)
    h = jnp.concatenate(hs, axis=1)
    hcar[...] = h
    hlast_ref[...] = h
    h_all = jnp.concatenate(
        [jnp.concatenate([h_s[j, bi * pitch:bi * pitch + tc, :] for bi in range(nb)], axis=0)
         for j in range(n_lane_tiles)], axis=1)
    out = h_all * jax.nn.gelu(y_ref[...].reshape(nb * tc, LRU_WIDTH))
    o_ref[...] = out.reshape(nb, tc, LRU_WIDTH).astype(BF16)


def lru_prompt(u3, conv_w, conv_b, wa, ba, wx, bx, lam, conv_buf, h0, layer, tc):
    nb, t, _ = u3.shape
    vec = pl.BlockSpec((None, 1, LRU_WIDTH), lambda i: (layer, 0, 0))
    mat = pl.BlockSpec((None, LRU_WIDTH, LRU_WIDTH), lambda i: (layer, 0, 0))
    return pl.pallas_call(
        functools.partial(_lru_prompt_kernel, nb=nb, tc=tc),
        out_shape=[jax.ShapeDtypeStruct((nb, t, LRU_WIDTH), BF16),
                   jax.ShapeDtypeStruct((nb, CONV_WIDTH - 1, LRU_WIDTH), F32),
                   jax.ShapeDtypeStruct((nb, LRU_WIDTH), F32)],
        grid=(t // tc,),
        in_specs=[
            pl.BlockSpec((nb, tc, LRU_WIDTH), lambda i: (0, i, U_X // LRU_WIDTH)),
            pl.BlockSpec((nb, tc, LRU_WIDTH), lambda i: (0, i, U_Y // LRU_WIDTH)),
            pl.BlockSpec((None, CONV_WIDTH, LRU_WIDTH), lambda i: (layer, 0, 0)),
            vec, mat, vec, mat, vec, vec,
            pl.BlockSpec((nb, 8, LRU_WIDTH), lambda i: (0, 0, 0)),
            pl.BlockSpec((nb, LRU_WIDTH), lambda i: (0, 0)),
        ],
        out_specs=[
            pl.BlockSpec((nb, tc, LRU_WIDTH), lambda i: (0, i, 0)),
            pl.BlockSpec((nb, CONV_WIDTH - 1, LRU_WIDTH), lambda i: (0, 0, 0)),
            pl.BlockSpec((nb, LRU_WIDTH), lambda i: (0, 0)),
        ],
        scratch_shapes=[pltpu.VMEM((nb, tc + 8, LRU_WIDTH), F32),
                        pltpu.VMEM((LRU_WIDTH // LANES, nb * (tc + LRU_ROW_PAD), LANES), F32),
                        pltpu.VMEM((LRU_WIDTH // LANES, nb * (tc + LRU_ROW_PAD), LANES), F32),
                        pltpu.VMEM((LRU_WIDTH // LANES, nb * (tc + LRU_ROW_PAD), LANES), F32),
                        pltpu.VMEM((nb, LRU_WIDTH), F32)],
        compiler_params=_params("arbitrary"),
        name="lru_prompt",
    )(u3, u3, conv_w, conv_b, wa, ba, wx, bx, lam, conv_buf, h0)


def _lru_sample_kernel(x_ref, y_ref, cw_ref, cb_ref, wa_ref, ba_ref, wx_ref, bx_ref, lam_ref, cbuf_ref, h0_ref,
                       o_ref, cnew_ref, hlast_ref, *, t_new):
    w = cw_ref[...]
    xx = [cbuf_ref[j] for j in range(CONV_WIDTH - 1)] + [x_ref[j] for j in range(t_new)]
    for j in range(CONV_WIDTH - 1):
        cnew_ref[j] = xx[t_new + j]
    h = h0_ref[...]
    for t in range(t_new):
        xf = cb_ref[...] + w[0:1, :] * xx[t]
        for j in range(1, CONV_WIDTH):
            xf = xf + w[j:j + 1, :] * xx[t + j]
        a, b = _lru_gates(xf, wa_ref[...], ba_ref[...], wx_ref[...], bx_ref[...], lam_ref[...])
        h = a * h + b
        o_ref[t] = (h * jax.nn.gelu(y_ref[t])).astype(BF16)
    hlast_ref[...] = h


def lru_sample(x_tm, y_tm, conv_w, conv_b, wa, ba, wx, bx, lam, conv_buf_tm, h0, layer):
    t_new, nb, _ = x_tm.shape
    vec = pl.BlockSpec((None, 1, LRU_WIDTH), lambda i: (layer, 0, 0))
    mat = pl.BlockSpec((None, LRU_WIDTH, LRU_WIDTH), lambda i: (layer, 0, 0))
    tok = pl.BlockSpec((t_new, nb, LRU_WIDTH), lambda i: (0, 0, 0))
    buf = pl.BlockSpec((CONV_WIDTH - 1, nb, LRU_WIDTH), lambda i: (0, 0, 0))
    st = pl.BlockSpec((nb, LRU_WIDTH), lambda i: (0, 0))
    return pl.pallas_call(
        functools.partial(_lru_sample_kernel, t_new=t_new),
        out_shape=[jax.ShapeDtypeStruct((t_new, nb, LRU_WIDTH), BF16),
                   jax.ShapeDtypeStruct((CONV_WIDTH - 1, nb, LRU_WIDTH), F32),
                   jax.ShapeDtypeStruct((nb, LRU_WIDTH), F32)],
        grid=(1,),
        in_specs=[tok, tok, pl.BlockSpec((None, CONV_WIDTH, LRU_WIDTH), lambda i: (layer, 0, 0)),
                  vec, mat, vec, mat, vec, vec, buf, st],
        out_specs=[tok, buf, st],
        compiler_params=_params("arbitrary"),
        name="lru_sample",
    )(x_tm, y_tm, conv_w, conv_b, wa, ba, wx, bx, lam, conv_buf_tm, h0)


def _merge_kernel(oa_ref, od_ref, or_ref, ga_ref, gd_ref, gr_ref, x_ref, wa_ref, wd_ref, wr_ref, wo_ref, o_ref):
    merged = (_sigmoid(ga_ref[...]) * _dot(oa_ref[...], wa_ref[...])
              + _sigmoid(gd_ref[...]) * _dot(od_ref[...], wd_ref[...])
              + _sigmoid(gr_ref[...]) * _dot(or_ref[...], wr_ref[...]))
    o_ref[...] = x_ref[...] + _dot(merged.astype(BF16), wo_ref[...])


def merge(o_a, o_d, o_r, u, x, w_a, w_d, w_r, w_out, layer, tm):
    m, d = x.shape
    br = lambda n: pl.BlockSpec((tm, n), lambda i: (i, 0))
    gate = lambda j: pl.BlockSpec((tm, d), lambda i: (i, j))
    wspec = lambda k: pl.BlockSpec((None, k, d), lambda i: (layer, 0, 0))
    return pl.pallas_call(
        _merge_kernel,
        out_shape=jax.ShapeDtypeStruct((m, d), F32),
        grid=(m // tm,),
        in_specs=[br(MOBA_WIDTH), br(GDN_WIDTH), br(LRU_WIDTH), gate(0), gate(1), gate(2), br(d),
                  wspec(MOBA_WIDTH), wspec(GDN_WIDTH), wspec(LRU_WIDTH), wspec(d)],
        out_specs=br(d),
        compiler_params=_params("parallel"),
        name="merge",
    )(o_a, o_d, o_r, u, u, u, x, w_a, w_d, w_r, w_out)


def _ffn_kernel(x_ref, g_ref, wu_ref, wd_ref, o_ref, h_ref, acc_ref):
    f = pl.program_id(1)

    @pl.when(f == 0)
    def _():
        h_ref[...] = _rms(x_ref[...], g_ref[...]).astype(BF16)
        acc_ref[...] = jnp.zeros_like(acc_ref)

    a = jnp.maximum(_dot(h_ref[...], wu_ref[...]), 0.0)
    acc_ref[...] += _dot((a * a).astype(BF16), wd_ref[...])

    @pl.when(f == pl.num_programs(1) - 1)
    def _():
        o_ref[...] = x_ref[...] + acc_ref[...]


def ffn(x, gain, w_up, w_down, layer, tm, tf):
    m, d = x.shape
    f = w_up.shape[-1]
    return pl.pallas_call(
        _ffn_kernel,
        out_shape=jax.ShapeDtypeStruct((m, d), F32),
        grid=(m // tm, f // tf),
        in_specs=[
            pl.BlockSpec((tm, d), lambda i, j: (i, 0)),
            pl.BlockSpec((None, 1, d), lambda i, j: (layer, 0, 0)),
            pl.BlockSpec((None, d, tf), lambda i, j: (layer, 0, j)),
            pl.BlockSpec((None, tf, d), lambda i, j: (layer, j, 0)),
        ],
        out_specs=pl.BlockSpec((tm, d), lambda i, j: (i, 0)),
        scratch_shapes=[pltpu.VMEM((tm, d), BF16), pltpu.VMEM((tm, d), F32)],
        compiler_params=_params("parallel", "arbitrary"),
        name="ffn",
    )(x, gain, w_up, w_down)


def _ple_kernel(x_ref, g_ref, wg_ref, p_ref, wp_ref, o_ref):
    x = x_ref[...]
    gate = _sigmoid(_dot(_rms(x, g_ref[...]).astype(BF16), wg_ref[...]))
    o_ref[...] = x + gate * _dot(p_ref[...].astype(BF16), wp_ref[...])


def ple(x, gain, w_gate, p, w_proj, layer, tm):
    m, d = x.shape
    steps = m // tm
    return pl.pallas_call(
        _ple_kernel,
        out_shape=jax.ShapeDtypeStruct((m, d), F32),
        grid=(steps,),
        in_specs=[
            pl.BlockSpec((tm, d), lambda i: (i, 0)),
            pl.BlockSpec((None, 1, d), lambda i: (layer, 0, 0)),
            pl.BlockSpec((None, d, d), lambda i: (layer, 0, 0)),
            pl.BlockSpec((tm, PLE_DIM), lambda i: (layer * steps + i, 0)),
            pl.BlockSpec((None, PLE_DIM, d), lambda i: (layer, 0, 0)),
        ],
        out_specs=pl.BlockSpec((tm, d), lambda i: (i, 0)),
        compiler_params=_params("parallel"),
        name="ple",
    )(x, gain, w_gate, p, w_proj)


def _rope_tables(pos):
    half = MOBA_HEAD_DIM // 2
    inv_freq = ROPE_THETA ** (-jnp.arange(half, dtype=F32) / half)
    ang = pos.astype(F32)[:, None] * inv_freq[None, :]
    cos = jnp.cos(ang)
    sin = jnp.sin(ang)
    cos_h = jnp.concatenate([cos, cos], axis=-1)
    sin_h = jnp.concatenate([-sin, sin], axis=-1)
    return jnp.tile(cos_h, (1, MOBA_HEADS)), jnp.tile(sin_h, (1, MOBA_HEADS))


def _block_diag(w):
    l, h, n, _ = w.shape
    eye = jnp.eye(h, dtype=w.dtype)
    return (w[:, :, :, None, :] * eye[None, :, None, :, None]).reshape(l, h * n, h * n)


def _row3(v):
    return v[:, None, :]


def _pad_lanes(v, n):
    return jnp.pad(v, ((0, 0), (0, n - v.shape[-1])))[:, None, :]


def kernel(x_prompt, x_sample, cache_k, cache_v, state_gdn, state_gdn_conv, state_lru_h, state_lru_conv,
           page_table, p_prompt, p_sample, g_mix, w_in, moba_q_norm, moba_k_norm, w_branch_a, gdn_conv_w,
           gdn_a_log, gdn_dt_bias, gdn_out_norm, w_branch_d, lru_conv_w, lru_conv_b, lru_wa, lru_ba, lru_wx,
           lru_bx, lru_lambda, w_branch_r, w_out, g_ffn, w_up, w_down, g_ple, w_ple_gate, w_ple_proj):
    depth = w_in.shape[0]
    bp, seq, d = x_prompt.shape
    bs, t_new, _ = x_sample.shape
    n_pages = page_table.shape[1]
    past_len = n_pages * PAGE_SIZE
    mp, ms = bp * seq, bs * t_new

    o = 0
    offs = []
    for size in (3 * MOBA_WIDTH, GDN_CONV_DIM, GDN_WIDTH, GDN_HEADS, GDN_HEADS, LRU_WIDTH, LRU_WIDTH, 3 * D_MODEL):
        offs.append((o, o + size))
        o += size
    (m0, m1), (d0, d1), (z0, z1), (a0, a1), (b0, b1), (x0, x1), (y0, y1), (g0, g1) = offs
    w_main = jnp.concatenate([w_in[:, :, g0:g1], w_in[:, :, m0:m1], w_in[:, :, d0:d1], w_in[:, :, z0:z1],
                              w_in[:, :, x0:x1], w_in[:, :, y0:y1]], axis=-1).astype(BF16)
    w_ab = jnp.pad(w_in[:, :, a0:b1], ((0, 0), (0, 0), (0, AB_COLS - 2 * GDN_HEADS))).astype(BF16)
    w_a16, w_d16, w_r16, w_o16 = (w.astype(BF16) for w in (w_branch_a, w_branch_d, w_branch_r, w_out))
    w_up16, w_down16, w_pg16, w_pp16 = (w.astype(BF16) for w in (w_up, w_down, w_ple_gate, w_ple_proj))
    wa_bd = _block_diag(lru_wa).astype(BF16)
    wx_bd = _block_diag(lru_wx).astype(BF16)
    head_mean = jnp.kron(jnp.eye(MOBA_HEADS, dtype=F32),
                         jnp.full((MOBA_HEAD_DIM, MOBA_HEAD_DIM), 1.0 / MOBA_HEAD_DIM, F32)).astype(BF16)
    gq = _row3(jnp.tile(moba_q_norm, (1, MOBA_HEADS)))
    gk = _row3(jnp.tile(moba_k_norm, (1, MOBA_HEADS)))
    g_mix3, g_ffn3, g_ple3 = _row3(g_mix), _row3(g_ffn), _row3(g_ple)
    a_log3 = _pad_lanes(gdn_a_log, AB_COLS)
    dt_bias3 = _pad_lanes(gdn_dt_bias, AB_COLS)
    out_norm3 = _row3(gdn_out_norm)
    lru_cb3, lru_ba3, lru_bx3, lru_lam3 = _row3(lru_conv_b), _row3(lru_ba), _row3(lru_bx), _row3(lru_lambda)

    cos_p, sin_p = _rope_tables(jnp.arange(seq, dtype=jnp.int32))
    cos_s, sin_s = _rope_tables(past_len + jnp.arange(t_new, dtype=jnp.int32))
    cos_s, sin_s = jnp.tile(cos_s, (bs, 1)), jnp.tile(sin_s, (bs, 1))

    cache_kt = cache_k.transpose(0, 2, 3, 4, 1)
    cache_vt = cache_v.transpose(0, 2, 3, 4, 1)
    pp = p_prompt.reshape(depth * mp, PLE_DIM)
    ps = p_sample.reshape(depth * ms, PLE_DIM)

    zero_gconv = jnp.zeros((bp, 8, GDN_CONV_DIM), F32)
    zero_gstate = jnp.zeros((bp, GDN_HEADS, GDN_DK, GDN_DV), F32)
    zero_lconv = jnp.zeros((bp, 8, LRU_WIDTH), F32)
    zero_lh = jnp.zeros((bp, LRU_WIDTH), F32)
    c_s = 16

    xp = x_prompt.reshape(mp, d)
    xs = x_sample.reshape(ms, d)
    outs = {k: [] for k in ("kp", "vp", "ks", "vs", "gsp", "gss", "gcp", "gcs", "lhp", "lhs", "lcp", "lcs")}

    for l in range(depth):
        u, ab = norm_matmul(xp, g_mix3, w_main, w_ab, l, 1024, 1536)
        q_a, k_a, v_a, kh = moba_prep(u, cos_p, sin_p, head_mean, gq, gk, l, MOBA_BLOCK, seq // MOBA_BLOCK, True)
        tok_p = (bp, seq, MOBA_WIDTH)
        o_a = moba_prompt(q_a.reshape(tok_p), kh, v_a.reshape(tok_p)).reshape(mp, MOBA_WIDTH)
        u3 = u.reshape(bp, seq, U_COLS)
        o_d, gconv, gstate = gdn(u3, ab.reshape(bp, seq, AB_COLS), gdn_conv_w, a_log3, dt_bias3, out_norm3,
                                 zero_gconv, zero_gstate, l, 4, GDN_CHUNK, GDN_CHUNK)
        o_r, lconv, lh = lru_prompt(u3, lru_conv_w, lru_cb3, wa_bd, lru_ba3, wx_bd, lru_bx3, lru_lam3,
                                    zero_lconv, zero_lh, l, 256)
        xp = merge(o_a, o_d.reshape(mp, GDN_WIDTH), o_r.reshape(mp, LRU_WIDTH), u, xp,
                   w_a16, w_d16, w_r16, w_o16, l, 256)
        xp = ffn(xp, g_ffn3, w_up16, w_down16, l, 1024, 1024)
        xp = ple(xp, g_ple3, w_pg16, pp, w_pp16, l, 512)
        outs["kp"].append(k_a); outs["vp"].append(v_a); outs["gcp"].append(gconv); outs["gsp"].append(gstate)
        outs["lcp"].append(lconv); outs["lhp"].append(lh)

        u, ab = norm_matmul(xs, g_mix3, w_main, w_ab, l, ms, 1536)
        q_s, k_s, v_s = moba_prep(u, cos_s, sin_s, head_mean, gq, gk, l, ms, 1, False)
        tok_s = (bs, t_new, MOBA_WIDTH)
        o_a = moba_sample(page_table, q_s.reshape(tok_s), k_s.reshape(tok_s), v_s.reshape(tok_s),
                          cache_kt, cache_vt, l)
        o_a = o_a.reshape(ms, MOBA_WIDTH).astype(BF16)
        pad_t = ((0, 0), (0, c_s - t_new), (0, 0))
        u3 = jnp.pad(u.reshape(bs, t_new, U_COLS), pad_t)
        ab3 = jnp.pad(ab.reshape(bs, t_new, AB_COLS), pad_t)
        gbuf = jnp.pad(state_gdn_conv[:, l], ((0, 0), (8 - (CONV_WIDTH - 1), 0), (0, 0)))
        o_d, gconv, gstate = gdn(u3, ab3, gdn_conv_w, a_log3, dt_bias3, out_norm3,
                                 gbuf, state_gdn, l, 8, c_s, t_new)
        o_d = o_d[:, :t_new].reshape(ms, GDN_WIDTH)
        us = u.reshape(bs, t_new, U_COLS)
        x_tm = us[:, :, U_X:U_X + LRU_WIDTH].transpose(1, 0, 2)
        y_tm = us[:, :, U_Y:U_Y + LRU_WIDTH].transpose(1, 0, 2)
        o_r, lconv, lh = lru_sample(x_tm, y_tm, lru_conv_w, lru_cb3, wa_bd, lru_ba3, wx_bd, lru_bx3, lru_lam3,
                                    state_lru_conv[:, l].transpose(1, 0, 2), state_lru_h[:, l], l)
        o_r = o_r.transpose(1, 0, 2).reshape(ms, LRU_WIDTH)
        xs = merge(o_a, o_d, o_r, u, xs, w_a16, w_d16, w_r16, w_o16, l, 256)
        xs = ffn(xs, g_ffn3, w_up16, w_down16, l, ms, 1024)
        xs = ple(xs, g_ple3, w_pg16, ps, w_pp16, l, ms)
        outs["ks"].append(k_s); outs["vs"].append(v_s); outs["gcs"].append(gconv); outs["gss"].append(gstate)
        outs["lcs"].append(lconv.transpose(1, 0, 2)); outs["lhs"].append(lh)

    hd = (MOBA_HEADS, MOBA_HEAD_DIM)
    k_prompt = jnp.stack(outs["kp"], axis=1).reshape(bp, seq, depth, *hd)
    v_prompt = jnp.stack(outs["vp"], axis=1).reshape(bp, seq, depth, *hd)
    k_sample = jnp.stack(outs["ks"], axis=1).reshape(bs, t_new, depth, *hd)
    v_sample = jnp.stack(outs["vs"], axis=1).reshape(bs, t_new, depth, *hd)
    return (xp.reshape(bp, seq, d), xs.reshape(bs, t_new, d), k_prompt, v_prompt, k_sample, v_sample,
            jnp.stack(outs["gsp"], axis=1), jnp.stack(outs["gss"], axis=1),
            jnp.stack(outs["gcp"], axis=1), jnp.stack(outs["gcs"], axis=1),
            jnp.stack(outs["lhp"], axis=1), jnp.stack(outs["lhs"], axis=1),
            jnp.stack(outs["lcp"], axis=1), jnp.stack(outs["lcs"], axis=1))
```

```python
import functools
import math

import jax
import jax.numpy as jnp
from jax import lax
from jax.experimental import pallas as pl
from jax.experimental.pallas import tpu as pltpu

F32 = jnp.float32
BF16 = jnp.bfloat16

D_MODEL = 1024
MOBA_HEADS = 8
MOBA_HEAD_DIM = 64
MOBA_WIDTH = MOBA_HEADS * MOBA_HEAD_DIM
MOBA_BLOCK = 256
MOBA_TOPK = 3
ROPE_THETA = 10000.0
PAGE_SIZE = 128
GDN_HEADS = 4
GDN_DK = 128
GDN_DV = 128
GDN_KEY_WIDTH = GDN_HEADS * GDN_DK
GDN_WIDTH = GDN_HEADS * GDN_DV
GDN_CONV_DIM = 2 * GDN_KEY_WIDTH + GDN_WIDTH
GDN_CHUNK = 64
CONV_WIDTH = 4
LRU_WIDTH = 512
LRU_HEADS = 8
LRU_BLOCK = LRU_WIDTH // LRU_HEADS
LRU_C = 8.0
D_FF = 4 * D_MODEL
PLE_DIM = 256
EPS = 1e-6

U_GATES = 0
U_MOBA = 3 * D_MODEL
U_GDN = U_MOBA + 3 * MOBA_WIDTH
U_Z = U_GDN + GDN_CONV_DIM
U_X = U_Z + GDN_WIDTH
U_Y = U_X + LRU_WIDTH
U_COLS = U_Y + LRU_WIDTH
AB_COLS = 128

VMEM_LIMIT = 56 * 1024 * 1024
LANES = 128
LOG2E = math.log2(math.e)
LRU_ROW_PAD = 8


def _params(*sem):
    return pltpu.CompilerParams(dimension_semantics=sem, vmem_limit_bytes=VMEM_LIMIT)


def _dot(a, b):
    return jnp.dot(a, b, preferred_element_type=F32)


def _dot_nt(a, b):
    return lax.dot_general(a, b, (((1,), (1,)), ((), ())), preferred_element_type=F32)


def _dot_tn(a, b):
    return lax.dot_general(a, b, (((0,), (0,)), ((), ())), preferred_element_type=F32)


def _split2(x):
    hi = x.astype(BF16)
    lo = (x - hi.astype(F32)).astype(BF16)
    return hi, lo


def _split3(x):
    hi = x.astype(BF16)
    r = x - hi.astype(F32)
    mid = r.astype(BF16)
    lo = (r - mid.astype(F32)).astype(BF16)
    return hi, mid, lo


def _dot_hp(a, b):
    ah, al = _split2(a)
    bh, bl = _split2(b)
    return _dot(ah, bh) + (_dot(ah, bl) + _dot(al, bh))


def _dot_lhs_exact(m_bf16, x):
    h, m, l = _split3(x)
    return _dot(m_bf16, h) + (_dot(m_bf16, m) + _dot(m_bf16, l))


def _dot_rhs_exact(x, m_bf16):
    h, m, l = _split3(x)
    return _dot(h, m_bf16) + (_dot(m, m_bf16) + _dot(l, m_bf16))


def _rms(xf, gain):
    ms = jnp.mean(xf * xf, axis=-1, keepdims=True)
    return xf * lax.rsqrt(ms + EPS) * gain


def _sigmoid(x):
    return 1.0 / (1.0 + jnp.exp(-x))


def _silu(x):
    return x * _sigmoid(x)


def _softplus(x):
    return jnp.maximum(x, 0.0) + jnp.log1p(jnp.exp(-jnp.abs(x)))


def _expm1(x):
    u = jnp.exp(x)
    um1 = u - 1.0
    lu = jnp.log(u)
    near = um1 * x / jnp.where(lu == 0.0, 1.0, lu)
    near = jnp.where(um1 == 0.0, x, near)
    return jnp.where(jnp.abs(x) < 0.5, near, um1)


def _norm_matmul_kernel(x_ref, g_ref, w_ref, w2_ref, o_ref, o2_ref, h_ref):
    @pl.when(pl.program_id(1) == 0)
    def _():
        h = _rms(x_ref[...], g_ref[...]).astype(BF16)
        h_ref[...] = h
        o2_ref[...] = _dot(h, w2_ref[...])

    o_ref[...] = _dot(h_ref[...], w_ref[...])


def norm_matmul(x, gain, w, w2, layer, tm, tn):
    m, d = x.shape
    n, n2 = w.shape[-1], w2.shape[-1]
    return pl.pallas_call(
        _norm_matmul_kernel,
        out_shape=[jax.ShapeDtypeStruct((m, n), F32), jax.ShapeDtypeStruct((m, n2), F32)],
        grid=(m // tm, n // tn),
        in_specs=[
            pl.BlockSpec((tm, d), lambda i, j: (i, 0)),
            pl.BlockSpec((None, 1, d), lambda i, j: (layer, 0, 0)),
            pl.BlockSpec((None, d, tn), lambda i, j: (layer, 0, j)),
            pl.BlockSpec((None, d, n2), lambda i, j: (layer, 0, 0)),
        ],
        out_specs=[pl.BlockSpec((tm, tn), lambda i, j: (i, j)),
                   pl.BlockSpec((tm, n2), lambda i, j: (i, 0))],
        scratch_shapes=[pltpu.VMEM((tm, d), BF16)],
        compiler_params=_params("parallel", "arbitrary"),
        name="norm_matmul",
    )(x, gain, w, w2)


def _moba_prep_kernel(qkv_ref, cos_ref, sin_ref, bd_ref, gq_ref, gk_ref, q_ref, k_ref, v_ref, *hm_refs):
    cos = cos_ref[...]
    sin = sin_ref[...]
    bd = bd_ref[...]
    lane = lax.broadcasted_iota(jnp.int32, cos.shape, 1)
    first_half = (lane % MOBA_HEAD_DIM) < (MOBA_HEAD_DIM // 2)

    def norm_rot(x, gain):
        ms = _dot_rhs_exact(x * x, bd)
        y = x * lax.rsqrt(ms + EPS) * gain
        partner = jnp.where(first_half,
                            pltpu.roll(y, MOBA_WIDTH - MOBA_HEAD_DIM // 2, 1),
                            pltpu.roll(y, MOBA_HEAD_DIM // 2, 1))
        return y * cos + partner * sin

    q = norm_rot(qkv_ref[:, 0:MOBA_WIDTH], gq_ref[...])
    k = norm_rot(qkv_ref[:, MOBA_WIDTH:2 * MOBA_WIDTH], gk_ref[...])
    v = qkv_ref[:, 2 * MOBA_WIDTH:3 * MOBA_WIDTH]
    q_ref[...] = q
    k_ref[...] = k
    v_ref[...] = v
    if hm_refs:
        kh_ref, = hm_refs
        for h in range(MOBA_HEADS):
            kh_ref[0, h] = k[:, h * MOBA_HEAD_DIM:(h + 1) * MOBA_HEAD_DIM].astype(BF16)


def moba_prep(u, cos, sin, bd, gq, gk, layer, tq, seq_blocks, head_major):
    m = u.shape[0]
    n_steps = m // tq
    out_shape = [jax.ShapeDtypeStruct((m, MOBA_WIDTH), F32)] * 3
    out_specs = [pl.BlockSpec((tq, MOBA_WIDTH), lambda i: (i, 0))] * 3
    if head_major:
        nb = n_steps // seq_blocks
        out_shape.append(jax.ShapeDtypeStruct((nb, MOBA_HEADS, seq_blocks * tq, MOBA_HEAD_DIM), BF16))
        out_specs.append(pl.BlockSpec((1, MOBA_HEADS, tq, MOBA_HEAD_DIM),
                                      lambda i: (i // seq_blocks, 0, i % seq_blocks, 0)))
    tab = pl.BlockSpec((tq, MOBA_WIDTH), lambda i: (i % seq_blocks, 0))
    vec = pl.BlockSpec((None, 1, MOBA_WIDTH), lambda i: (layer, 0, 0))
    return pl.pallas_call(
        _moba_prep_kernel,
        out_shape=out_shape,
        grid=(n_steps,),
        in_specs=[
            pl.BlockSpec((tq, 3 * MOBA_WIDTH), lambda i: (i, U_MOBA // (3 * MOBA_WIDTH))),
            tab, tab,
            pl.BlockSpec((MOBA_WIDTH, MOBA_WIDTH), lambda i: (0, 0)),
            vec, vec,
        ],
        out_specs=out_specs,
        compiler_params=_params("parallel"),
        name="moba_prep",
    )(u, cos, sin, bd, gq, gk)


def _topk_select(gates):
    n = len(gates)
    if n <= MOBA_TOPK:
        return [None] * n
    sel = []
    for a in range(n):
        rank = jnp.zeros(gates[a].shape, F32)
        for b in range(n):
            if b == a:
                continue
            beats = (gates[b] >= gates[a]) if b < a else (gates[b] > gates[a])
            rank = rank + jnp.where(beats, 1.0, 0.0)
        sel.append(rank < float(MOBA_TOPK))
    return sel


HEADS_PER_STEP = LANES // MOBA_HEAD_DIM


def _moba_prompt_kernel(q_ref, k_ref, v_ref, o_ref, *, n_blk):
    blk = MOBA_BLOCK
    dh = MOBA_HEAD_DIM
    scale = dh ** -0.5
    key = lax.broadcasted_iota(jnp.int32, (blk, blk), 0)
    qry = lax.broadcasted_iota(jnp.int32, (blk, blk), 1)
    causal = key <= qry
    vt = [v_ref[0, j * blk:(j + 1) * blk, :].T.astype(BF16) for j in range(n_blk)]
    def scores(i, hh):
        qt = q_ref[0, i * blk:(i + 1) * blk, :].T.astype(BF16)
        return _dot(k_ref[0, hh, 0:(i + 1) * blk, :], qt[hh * dh:(hh + 1) * dh, :])

    def attend(i, hh, s):
        parts = [s[j * blk:(j + 1) * blk, :] for j in range(i + 1)]
        gates = [jnp.sum(parts[j], axis=0, keepdims=True) for j in range(i)]
        sel = _topk_select(gates)
        masked = []
        for j in range(i):
            masked.append(parts[j] if sel[j] is None else jnp.where(sel[j], parts[j], -jnp.inf))
        masked.append(jnp.where(causal, parts[i], -jnp.inf))
        mx = masked[0].max(axis=0, keepdims=True)
        for j in range(1, i + 1):
            mx = jnp.maximum(mx, masked[j].max(axis=0, keepdims=True))
        den = jnp.zeros((1, blk), F32)
        acc = jnp.zeros((dh, blk), F32)
        for j in range(i + 1):
            p = jnp.exp2((masked[j] - mx) * (scale * LOG2E))
            den = den + jnp.sum(p, axis=0, keepdims=True)
            acc = acc + _dot(vt[j][hh * dh:(hh + 1) * dh, :], p.astype(BF16))
        return acc / den

    units = [(i, hh) for i in range(n_blk) for hh in range(HEADS_PER_STEP)]
    s_next = scores(*units[0])
    outs = []
    for n, (i, hh) in enumerate(units):
        s_cur = s_next
        if n + 1 < len(units):
            s_next = scores(*units[n + 1])
        outs.append(attend(i, hh, s_cur))
        if hh == HEADS_PER_STEP - 1:
            o_ref[0, i * blk:(i + 1) * blk, :] = jnp.concatenate(outs, axis=0).T.astype(BF16)
            outs = []


def moba_prompt(q, kh, v):
    nb, nh, t, dh = kh.shape
    tok = pl.BlockSpec((1, t, HEADS_PER_STEP * dh), lambda b, g: (b, 0, g))
    return pl.pallas_call(
        functools.partial(_moba_prompt_kernel, n_blk=t // MOBA_BLOCK),
        out_shape=jax.ShapeDtypeStruct((nb, t, nh * dh), BF16),
        grid=(nb, nh // HEADS_PER_STEP),
        in_specs=[tok, pl.BlockSpec((1, HEADS_PER_STEP, t, dh), lambda b, g: (b, g, 0, 0)), tok],
        out_specs=tok,
        compiler_params=_params("parallel", "parallel"),
        name="moba_prompt",
    )(q, kh, v)


def _moba_sample_kernel(pt_ref, q_ref, kn_ref, vn_ref, *refs, n_pages, t_new):
    del pt_ref
    k_refs, v_refs, o_ref = refs[:n_pages], refs[n_pages:2 * n_pages], refs[2 * n_pages]
    rows = t_new * MOBA_HEADS
    scale = MOBA_HEAD_DIM ** -0.5
    q = q_ref[0]
    head_of_row = lax.broadcasted_iota(jnp.int32, (MOBA_HEADS, MOBA_WIDTH), 0)
    head_of_lane = lax.broadcasted_iota(jnp.int32, (MOBA_HEADS, MOBA_WIDTH), 1) // MOBA_HEAD_DIM
    own_head = head_of_row == head_of_lane
    q_bd = jnp.concatenate(
        [jnp.where(own_head, jnp.broadcast_to(q[t:t + 1, :], (MOBA_HEADS, MOBA_WIDTH)), 0.0) for t in range(t_new)],
        axis=0)
    q16 = q_bd.astype(BF16)
    pages = [_dot(q16, k_refs[j][...].reshape(MOBA_WIDTH, PAGE_SIZE).astype(BF16)) for j in range(n_pages)]

    pages_per_blk = MOBA_BLOCK // PAGE_SIZE
    n_past = n_pages // pages_per_blk
    gates = []
    for n in range(n_past):
        g = jnp.sum(pages[n * pages_per_blk], axis=-1, keepdims=True)
        for r in range(1, pages_per_blk):
            g = g + jnp.sum(pages[n * pages_per_blk + r], axis=-1, keepdims=True)
        gates.append(g)
    sel = _topk_select(gates)
    masked = []
    for j in range(n_pages):
        s_n = sel[j // pages_per_blk]
        masked.append(pages[j] if s_n is None else jnp.where(s_n, pages[j], -jnp.inf))
    tok_of_row = lax.broadcasted_iota(jnp.int32, (rows, 1), 0) // MOBA_HEADS
    kn = kn_ref[0]
    vn = vn_ref[0]
    own = []
    for j in range(t_new):
        s_j = jnp.sum(q_bd * kn[j:j + 1, :], axis=-1, keepdims=True)
        own.append(jnp.where(tok_of_row >= j, s_j, -jnp.inf))
    mx = own[0]
    for j in range(1, t_new):
        mx = jnp.maximum(mx, own[j])
    for j in range(n_pages):
        mx = jnp.maximum(mx, masked[j].max(axis=-1, keepdims=True))
    den = jnp.zeros((rows, 1), F32)
    acc = jnp.zeros((rows, MOBA_WIDTH), F32)
    for j in range(t_new):
        pj = jnp.exp((own[j] - mx) * scale)
        den = den + pj
        acc = acc + pj * vn[j:j + 1, :]
    for j in range(n_pages):
        pj = jnp.exp((masked[j] - mx) * scale)
        den = den + jnp.sum(pj, axis=-1, keepdims=True)
        acc = acc + _dot_nt(pj.astype(BF16), v_refs[j][...].reshape(MOBA_WIDTH, PAGE_SIZE).astype(BF16))
    acc = acc / den
    outs = []
    for t in range(t_new):
        a_t = acc[t * MOBA_HEADS:(t + 1) * MOBA_HEADS, :]
        outs.append(jnp.sum(jnp.where(own_head, a_t, 0.0), axis=0, keepdims=True))
    o_ref[0] = jnp.concatenate(outs, axis=0)


def moba_sample(page_table, q, k_new, v_new, cache_kt, cache_vt, layer):
    nb, t_new, _ = q.shape
    n_pages = page_table.shape[1]
    tok = pl.BlockSpec((1, t_new, MOBA_WIDTH), lambda b, pt: (b, 0, 0))

    def page(j):
        return pl.BlockSpec((None, None, MOBA_HEADS, MOBA_HEAD_DIM, PAGE_SIZE),
                            lambda b, pt: (pt[b * n_pages + j], layer, 0, 0, 0))

    pages = [page(j) for j in range(n_pages)]
    return pl.pallas_call(
        functools.partial(_moba_sample_kernel, n_pages=n_pages, t_new=t_new),
        out_shape=jax.ShapeDtypeStruct((nb, t_new, MOBA_WIDTH), F32),
        grid_spec=pltpu.PrefetchScalarGridSpec(
            num_scalar_prefetch=1,
            grid=(nb,),
            in_specs=[tok, tok, tok] + pages + pages,
            out_specs=tok,
        ),
        compiler_params=_params("parallel"),
        name="moba_sample",
    )(page_table.reshape(-1), q, k_new, v_new, *([cache_kt] * n_pages), *([cache_vt] * n_pages))


def _unit_lower_inverse(lows, c):
    row = lax.broadcasted_iota(jnp.int32, (c, c), 0)
    col = lax.broadcasted_iota(jnp.int32, (c, c), 1)
    eye = jnp.where(row == col, 1.0, 0.0)
    pair = row // 2 == col // 2
    xs = [eye - jnp.where(pair, low, 0.0) for low in lows]
    s = 2
    while s < c:
        sub = (row // (2 * s) == col // (2 * s)) & (row // s != col // s)
        xe = [_dot_hp(x, jnp.where(sub, low, 0.0)) for x, low in zip(xs, lows)]
        xs = [x - _dot_hp(t, x) for x, t in zip(xs, xe)]
        s *= 2
    return xs


def _gdn_kernel(qkv_ref, z_ref, ab_ref, cw_ref, alog_ref, dtb_ref, gn_ref, cbuf_ref, s0_ref,
                o_ref, cnew_ref, snew_ref, xx_ref, st_ref, *, bb, c, t_valid):
    ci = pl.program_id(1)

    @pl.when(ci == 0)
    def _():
        xx_ref[:, 0:8, :] = cbuf_ref[...]
        st_ref[...] = s0_ref[...]

    xx_ref[:, 8:8 + c, :] = qkv_ref[...]
    w = cw_ref[...]
    masked = t_valid < c
    valid = lax.broadcasted_iota(jnp.int32, (c, 1), 0) < t_valid
    row = lax.broadcasted_iota(jnp.int32, (c, c), 0)
    col = lax.broadcasted_iota(jnp.int32, (c, c), 1)
    tri = row >= col
    eye = row == col
    tri16 = jnp.where(tri, 1.0, 0.0).astype(BF16)
    gn = gn_ref[...]

    ys, gcs, betas = [], [], []
    for bi in range(bb):
        y = (w[0:1, :] * xx_ref[bi, 5:5 + c, :] + w[1:2, :] * xx_ref[bi, 6:6 + c, :]
             + w[2:3, :] * xx_ref[bi, 7:7 + c, :] + w[3:4, :] * xx_ref[bi, 8:8 + c, :])
        cnew_ref[bi] = xx_ref[bi, 5 + t_valid:8 + t_valid, :]
        xx_ref[bi, 0:8, :] = xx_ref[bi, c:c + 8, :]
        ys.append(_silu(y))
        ab = ab_ref[bi]
        g_all = -jnp.exp(alog_ref[...]) * _softplus(ab + dtb_ref[...])
        if masked:
            g_all = jnp.where(valid, g_all, 0.0)
        gcs.append(_dot_lhs_exact(tri16, g_all))
        betas.append(_sigmoid(ab))

    chains = [(bi, h) for bi in range(bb) for h in range(GDN_HEADS)]
    q_l, k_l, kb_l, vb_l, gcol_l, glast_l, decay_l = [], [], [], [], [], [], []
    for bi, h in chains:
        lo, hi = h * GDN_DK, (h + 1) * GDN_DK
        y = ys[bi]
        q = y[:, lo:hi]
        k = y[:, GDN_KEY_WIDTH + lo:GDN_KEY_WIDTH + hi]
        v = y[:, 2 * GDN_KEY_WIDTH + lo:2 * GDN_KEY_WIDTH + hi]
        q = q * lax.rsqrt(jnp.sum(q * q, axis=-1, keepdims=True) + EPS) * (GDN_DK ** -0.5)
        k = k * lax.rsqrt(jnp.sum(k * k, axis=-1, keepdims=True) + EPS)
        beta = betas[bi][:, GDN_HEADS + h:GDN_HEADS + h + 1]
        if masked:
            k = jnp.where(valid, k, 0.0)
            v = jnp.where(valid, v, 0.0)
            beta = jnp.where(valid, beta, 0.0)
        g_col = gcs[bi][:, h:h + 1]
        g_row = jnp.sum(jnp.where(eye, g_col, 0.0), axis=0, keepdims=True)
        q_l.append(q)
        k_l.append(k)
        kb_l.append(k * beta)
        vb_l.append(v * beta)
        gcol_l.append(g_col)
        glast_l.append(gcs[bi][c - 1:c, h:h + 1])
        decay_l.append(jnp.exp(jnp.where(tri, g_col - g_row, -jnp.inf)))
    k16_l = [k.astype(BF16) for k in k_l]
    low_l = [jnp.where(row > col, _dot_nt(kb.astype(BF16), k16) * decay, 0.0)
             for kb, k16, decay in zip(kb_l, k16_l, decay_l)]
    a_l = [(_dot_nt(q.astype(BF16), k16) * decay).astype(BF16) for q, k16, decay in zip(q_l, k16_l, decay_l)]
    t_l = [t.astype(BF16) for t in _unit_lower_inverse(low_l, c)]
    eg_l = [jnp.exp(g) for g in gcol_l]
    u_l = [_dot(t, vb.astype(BF16)) for t, vb in zip(t_l, vb_l)]
    wk_l = [_dot(t, (kb * eg).astype(BF16)).astype(BF16) for t, kb, eg in zip(t_l, kb_l, eg_l)]
    qd_l = [(q * eg).astype(BF16) for q, eg in zip(q_l, eg_l)]
    kd_l = [(k * jnp.exp(gl - g)).astype(BF16) for k, gl, g in zip(k_l, glast_l, gcol_l)]
    st_l = [st_ref[bi, h] for bi, h in chains]
    st16_l = [st.astype(BF16) for st in st_l]
    vn_l = [(u - _dot(wk, st16)).astype(BF16) for u, wk, st16 in zip(u_l, wk_l, st16_l)]
    o_l = [_dot(qd, st16) + _dot(a, vn) for qd, st16, a, vn in zip(qd_l, st16_l, a_l, vn_l)]
    sn_l = [st * jnp.exp(gl) + _dot_tn(kd, vn) for st, gl, kd, vn in zip(st_l, glast_l, kd_l, vn_l)]
    for (bi, h), o, sn in zip(chains, o_l, sn_l):
        lo, hi = h * GDN_DK, (h + 1) * GDN_DK
        st_ref[bi, h] = sn
        snew_ref[bi, h] = sn
        on = _rms(o, gn) * _silu(z_ref[bi, :, lo:hi])
        o_ref[bi, :, lo:hi] = on.astype(BF16)


def gdn(u3, ab3, conv_w, a_log, dt_bias, out_norm, conv_buf, s0, layer, bb, c, t_valid):
    nb, t, _ = u3.shape
    vec = lambda n: pl.BlockSpec((None, 1, n), lambda b, i: (layer, 0, 0))
    return pl.pallas_call(
        functools.partial(_gdn_kernel, bb=bb, c=c, t_valid=t_valid),
        out_shape=[jax.ShapeDtypeStruct((nb, t, GDN_WIDTH), BF16),
                   jax.ShapeDtypeStruct((nb, CONV_WIDTH - 1, GDN_CONV_DIM), F32),
                   jax.ShapeDtypeStruct((nb, GDN_HEADS, GDN_DK, GDN_DV), F32)],
        grid=(nb // bb, t // c),
        in_specs=[
            pl.BlockSpec((bb, c, GDN_CONV_DIM), lambda b, i: (b, i, U_GDN // GDN_CONV_DIM)),
            pl.BlockSpec((bb, c, GDN_WIDTH), lambda b, i: (b, i, U_Z // GDN_WIDTH)),
            pl.BlockSpec((bb, c, AB_COLS), lambda b, i: (b, i, 0)),
            pl.BlockSpec((None, CONV_WIDTH, GDN_CONV_DIM), lambda b, i: (layer, 0, 0)),
            vec(AB_COLS), vec(AB_COLS), vec(GDN_DV),
            pl.BlockSpec((bb, 8, GDN_CONV_DIM), lambda b, i: (b, 0, 0)),
            (pl.BlockSpec((bb, GDN_HEADS, GDN_DK, GDN_DV), lambda b, i: (b, 0, 0, 0)) if s0.ndim == 4 else
             pl.BlockSpec((bb, None, GDN_HEADS, GDN_DK, GDN_DV), lambda b, i: (b, layer, 0, 0, 0))),
        ],
        out_specs=[
            pl.BlockSpec((bb, c, GDN_WIDTH), lambda b, i: (b, i, 0)),
            pl.BlockSpec((bb, CONV_WIDTH - 1, GDN_CONV_DIM), lambda b, i: (b, 0, 0)),
            pl.BlockSpec((bb, GDN_HEADS, GDN_DK, GDN_DV), lambda b, i: (b, 0, 0, 0)),
        ],
        scratch_shapes=[pltpu.VMEM((bb, c + 8, GDN_CONV_DIM), F32),
                        pltpu.VMEM((bb, GDN_HEADS, GDN_DK, GDN_DV), F32)],
        compiler_params=_params("parallel", "arbitrary"),
        name="gdn",
    )(u3, u3, ab3, conv_w, a_log, dt_bias, out_norm, conv_buf, s0)


def _lru_gates(xf, wa, ba, wx, bx, lam):
    x16 = xf.astype(BF16)
    r = _sigmoid(_dot(x16, wa) + ba)
    i = _sigmoid(_dot(x16, wx) + bx)
    log_a = -LRU_C * r * _softplus(-lam)
    a = jnp.exp(log_a)
    b = jnp.sqrt(-_expm1(2.0 * log_a)) * (i * xf)
    return a, b


def _lru_prompt_kernel(x_ref, y_ref, cw_ref, cb_ref, wa_ref, ba_ref, wx_ref, bx_ref, lam_ref, cbuf_ref, h0_ref,
                       o_ref, cnew_ref, hlast_ref, xx_ref, a_s, b_s, h_s, hcar, *, nb, tc):
    ci = pl.program_id(0)

    @pl.when(ci == 0)
    def _():
        xx_ref[:, 0:8, :] = cbuf_ref[...]
        hcar[...] = h0_ref[...]

    xx_ref[:, 8:8 + tc, :] = x_ref[...]
    w = cw_ref[...]
    xc = (w[0:1, :] * xx_ref[:, 5:5 + tc, :] + w[1:2, :] * xx_ref[:, 6:6 + tc, :]
          + w[2:3, :] * xx_ref[:, 7:7 + tc, :] + w[3:4, :] * xx_ref[:, 8:8 + tc, :]) + cb_ref[...]
    cnew_ref[...] = xx_ref[:, tc + 5:tc + 8, :]
    xx_ref[:, 0:8, :] = xx_ref[:, tc:tc + 8, :]
    xf = xc.reshape(nb * tc, LRU_WIDTH)
    a, b = _lru_gates(xf, wa_ref[...], ba_ref[...], wx_ref[...], bx_ref[...], lam_ref[...])
    n_lane_tiles = LRU_WIDTH // LANES
    pitch = tc + LRU_ROW_PAD
    for j in range(n_lane_tiles):
        for bi in range(nb):
            a_s[j, bi * pitch:bi * pitch + tc, :] = a[bi * tc:(bi + 1) * tc, j * LANES:(j + 1) * LANES]
            b_s[j, bi * pitch:bi * pitch + tc, :] = b[bi * tc:(bi + 1) * tc, j * LANES:(j + 1) * LANES]

    def step(t, hs):
        new = []
        for j in range(n_lane_tiles):
            h = a_s[j, pl.ds(t, nb, stride=pitch), :] * hs[j] + b_s[j, pl.ds(t, nb, stride=pitch), :]
            h_s[j, pl.ds(t, nb, stride=pitch), :] = h
            new.append(h)
        return tuple(new)

    h0 = hcar[...]
    hs = lax.fori_loop(0, tc, step, tuple(h0[:, j * LANES:(j + 1) * LANES] for j in range(n_lane_tiles)),
                       unroll=8)
    h = jnp.concatenate(hs, axis=1)
    hcar[...] = h
    hlast_ref[...] = h
    h_all = jnp.concatenate(
        [jnp.concatenate([h_s[j, bi * pitch:bi * pitch + tc, :] for bi in range(nb)], axis=0)
         for j in range(n_lane_tiles)], axis=1)
    out = h_all * jax.nn.gelu(y_ref[...].reshape(nb * tc, LRU_WIDTH))
    o_ref[...] = out.reshape(nb, tc, LRU_WIDTH).astype(BF16)


def lru_prompt(u3, conv_w, conv_b, wa, ba, wx, bx, lam, conv_buf, h0, layer, tc):
    nb, t, _ = u3.shape
    vec = pl.BlockSpec((None, 1, LRU_WIDTH), lambda i: (layer, 0, 0))
    mat = pl.BlockSpec((None, LRU_WIDTH, LRU_WIDTH), lambda i: (layer, 0, 0))
    return pl.pallas_call(
        functools.partial(_lru_prompt_kernel, nb=nb, tc=tc),
        out_shape=[jax.ShapeDtypeStruct((nb, t, LRU_WIDTH), BF16),
                   jax.ShapeDtypeStruct((nb, CONV_WIDTH - 1, LRU_WIDTH), F32),
                   jax.ShapeDtypeStruct((nb, LRU_WIDTH), F32)],
        grid=(t // tc,),
        in_specs=[
            pl.BlockSpec((nb, tc, LRU_WIDTH), lambda i: (0, i, U_X // LRU_WIDTH)),
            pl.BlockSpec((nb, tc, LRU_WIDTH), lambda i: (0, i, U_Y // LRU_WIDTH)),
            pl.BlockSpec((None, CONV_WIDTH, LRU_WIDTH), lambda i: (layer, 0, 0)),
            vec, mat, vec, mat, vec, vec,
            pl.BlockSpec((nb, 8, LRU_WIDTH), lambda i: (0, 0, 0)),
            pl.BlockSpec((nb, LRU_WIDTH), lambda i: (0, 0)),
        ],
        out_specs=[
            pl.BlockSpec((nb, tc, LRU_WIDTH), lambda i: (0, i, 0)),
            pl.BlockSpec((nb, CONV_WIDTH - 1, LRU_WIDTH), lambda i: (0, 0, 0)),
            pl.BlockSpec((nb, LRU_WIDTH), lambda i: (0, 0)),
        ],
        scratch_shapes=[pltpu.VMEM((nb, tc + 8, LRU_WIDTH), F32),
                        pltpu.VMEM((LRU_WIDTH // LANES, nb * (tc + LRU_ROW_PAD), LANES), F32),
                        pltpu.VMEM((LRU_WIDTH // LANES, nb * (tc + LRU_ROW_PAD), LANES), F32),
                        pltpu.VMEM((LRU_WIDTH // LANES, nb * (tc + LRU_ROW_PAD), LANES), F32),
                        pltpu.VMEM((nb, LRU_WIDTH), F32)],
        compiler_params=_params("arbitrary"),
        name="lru_prompt",
    )(u3, u3, conv_w, conv_b, wa, ba, wx, bx, lam, conv_buf, h0)


def _lru_sample_kernel(x_ref, y_ref, cw_ref, cb_ref, wa_ref, ba_ref, wx_ref, bx_ref, lam_ref, cbuf_ref, h0_ref,
                       o_ref, cnew_ref, hlast_ref, *, t_new):
    w = cw_ref[...]
    xx = [cbuf_ref[j] for j in range(CONV_WIDTH - 1)] + [x_ref[j] for j in range(t_new)]
    for j in range(CONV_WIDTH - 1):
        cnew_ref[j] = xx[t_new + j]
    h = h0_ref[...]
    for t in range(t_new):
        xf = cb_ref[...] + w[0:1, :] * xx[t]
        for j in range(1, CONV_WIDTH):
            xf = xf + w[j:j + 1, :] * xx[t + j]
        a, b = _lru_gates(xf, wa_ref[...], ba_ref[...], wx_ref[...], bx_ref[...], lam_ref[...])
        h = a * h + b
        o_ref[t] = (h * jax.nn.gelu(y_ref[t])).astype(BF16)
    hlast_ref[...] = h


def lru_sample(x_tm, y_tm, conv_w, conv_b, wa, ba, wx, bx, lam, conv_buf_tm, h0, layer):
    t_new, nb, _ = x_tm.shape
    vec = pl.BlockSpec((None, 1, LRU_WIDTH), lambda i: (layer, 0, 0))
    mat = pl.BlockSpec((None, LRU_WIDTH, LRU_WIDTH), lambda i: (layer, 0, 0))
    tok = pl.BlockSpec((t_new, nb, LRU_WIDTH), lambda i: (0, 0, 0))
    buf = pl.BlockSpec((CONV_WIDTH - 1, nb, LRU_WIDTH), lambda i: (0, 0, 0))
    st = pl.BlockSpec((nb, LRU_WIDTH), lambda i: (0, 0))
    return pl.pallas_call(
        functools.partial(_lru_sample_kernel, t_new=t_new),
        out_shape=[jax.ShapeDtypeStruct((t_new, nb, LRU_WIDTH), BF16),
                   jax.ShapeDtypeStruct((CONV_WIDTH - 1, nb, LRU_WIDTH), F32),
                   jax.ShapeDtypeStruct((nb, LRU_WIDTH), F32)],
        grid=(1,),
        in_specs=[tok, tok, pl.BlockSpec((None, CONV_WIDTH, LRU_WIDTH), lambda i: (layer, 0, 0)),
                  vec, mat, vec, mat, vec, vec, buf, st],
        out_specs=[tok, buf, st],
        compiler_params=_params("arbitrary"),
        name="lru_sample",
    )(x_tm, y_tm, conv_w, conv_b, wa, ba, wx, bx, lam, conv_buf_tm, h0)


def _merge_kernel(oa_ref, od_ref, or_ref, ga_ref, gd_ref, gr_ref, x_ref, wa_ref, wd_ref, wr_ref, wo_ref, o_ref):
    merged = (_sigmoid(ga_ref[...]) * _dot(oa_ref[...], wa_ref[...])
              + _sigmoid(gd_ref[...]) * _dot(od_ref[...], wd_ref[...])
              + _sigmoid(gr_ref[...]) * _dot(or_ref[...], wr_ref[...]))
    o_ref[...] = x_ref[...] + _dot(merged.astype(BF16), wo_ref[...])


def merge(o_a, o_d, o_r, u, x, w_a, w_d, w_r, w_out, layer, tm):
    m, d = x.shape
    br = lambda n: pl.BlockSpec((tm, n), lambda i: (i, 0))
    gate = lambda j: pl.BlockSpec((tm, d), lambda i: (i, j))
    wspec = lambda k: pl.BlockSpec((None, k, d), lambda i: (layer, 0, 0))
    return pl.pallas_call(
        _merge_kernel,
        out_shape=jax.ShapeDtypeStruct((m, d), F32),
        grid=(m // tm,),
        in_specs=[br(MOBA_WIDTH), br(GDN_WIDTH), br(LRU_WIDTH), gate(0), gate(1), gate(2), br(d),
                  wspec(MOBA_WIDTH), wspec(GDN_WIDTH), wspec(LRU_WIDTH), wspec(d)],
        out_specs=br(d),
        compiler_params=_params("parallel"),
        name="merge",
    )(o_a, o_d, o_r, u, u, u, x, w_a, w_d, w_r, w_out)


def _ffn_ple_kernel(x_ref, gf_ref, wu_ref, wd_ref, gp_ref, wg_ref, p_ref, wp_ref, o_ref, h_ref, acc_ref):
    f = pl.program_id(1)

    @pl.when(f == 0)
    def _():
        h_ref[...] = _rms(x_ref[...], gf_ref[...]).astype(BF16)
        acc_ref[...] = jnp.zeros_like(acc_ref)

    a = jnp.maximum(_dot(h_ref[...], wu_ref[...]), 0.0)
    acc_ref[...] += _dot((a * a).astype(BF16), wd_ref[...])

    @pl.when(f == pl.num_programs(1) - 1)
    def _():
        x = x_ref[...] + acc_ref[...]
        gate = _sigmoid(_dot(_rms(x, gp_ref[...]).astype(BF16), wg_ref[...]))
        o_ref[...] = x + gate * _dot(p_ref[...].astype(BF16), wp_ref[...])


def ffn_ple(x, g_ffn, w_up, w_down, g_ple, w_gate, p, w_proj, layer, tm, tf):
    m, d = x.shape
    f = w_up.shape[-1]
    steps = m // tm
    vec = pl.BlockSpec((None, 1, d), lambda i, j: (layer, 0, 0))
    return pl.pallas_call(
        _ffn_ple_kernel,
        out_shape=jax.ShapeDtypeStruct((m, d), F32),
        grid=(steps, f // tf),
        in_specs=[
            pl.BlockSpec((tm, d), lambda i, j: (i, 0)),
            vec,
            pl.BlockSpec((None, d, tf), lambda i, j: (layer, 0, j)),
            pl.BlockSpec((None, tf, d), lambda i, j: (layer, j, 0)),
            vec,
            pl.BlockSpec((None, d, d), lambda i, j: (layer, 0, 0)),
            pl.BlockSpec((tm, PLE_DIM), lambda i, j: (layer * steps + i, 0)),
            pl.BlockSpec((None, PLE_DIM, d), lambda i, j: (layer, 0, 0)),
        ],
        out_specs=pl.BlockSpec((tm, d), lambda i, j: (i, 0)),
        scratch_shapes=[pltpu.VMEM((tm, d), BF16), pltpu.VMEM((tm, d), F32)],
        compiler_params=_params("parallel", "arbitrary"),
        name="ffn_ple",
    )(x, g_ffn, w_up, w_down, g_ple, w_gate, p, w_proj)


def _rope_tables(pos):
    half = MOBA_HEAD_DIM // 2
    inv_freq = ROPE_THETA ** (-jnp.arange(half, dtype=F32) / half)
    ang = pos.astype(F32)[:, None] * inv_freq[None, :]
    cos = jnp.cos(ang)
    sin = jnp.sin(ang)
    cos_h = jnp.concatenate([cos, cos], axis=-1)
    sin_h = jnp.concatenate([-sin, sin], axis=-1)
    return jnp.tile(cos_h, (1, MOBA_HEADS)), jnp.tile(sin_h, (1, MOBA_HEADS))


def _block_diag(w):
    l, h, n, _ = w.shape
    eye = jnp.eye(h, dtype=w.dtype)
    return (w[:, :, :, None, :] * eye[None, :, None, :, None]).reshape(l, h * n, h * n)


def _row3(v):
    return v[:, None, :]


def _pad_lanes(v, n):
    return jnp.pad(v, ((0, 0), (0, n - v.shape[-1])))[:, None, :]


def kernel(x_prompt, x_sample, cache_k, cache_v, state_gdn, state_gdn_conv, state_lru_h, state_lru_conv,
           page_table, p_prompt, p_sample, g_mix, w_in, moba_q_norm, moba_k_norm, w_branch_a, gdn_conv_w,
           gdn_a_log, gdn_dt_bias, gdn_out_norm, w_branch_d, lru_conv_w, lru_conv_b, lru_wa, lru_ba, lru_wx,
           lru_bx, lru_lambda, w_branch_r, w_out, g_ffn, w_up, w_down, g_ple, w_ple_gate, w_ple_proj):
    depth = w_in.shape[0]
    bp, seq, d = x_prompt.shape
    bs, t_new, _ = x_sample.shape
    n_pages = page_table.shape[1]
    past_len = n_pages * PAGE_SIZE
    mp, ms = bp * seq, bs * t_new

    o = 0
    offs = []
    for size in (3 * MOBA_WIDTH, GDN_CONV_DIM, GDN_WIDTH, GDN_HEADS, GDN_HEADS, LRU_WIDTH, LRU_WIDTH, 3 * D_MODEL):
        offs.append((o, o + size))
        o += size
    (m0, m1), (d0, d1), (z0, z1), (a0, a1), (b0, b1), (x0, x1), (y0, y1), (g0, g1) = offs
    w_main = jnp.concatenate([w_in[:, :, g0:g1], w_in[:, :, m0:m1], w_in[:, :, d0:d1], w_in[:, :, z0:z1],
                              w_in[:, :, x0:x1], w_in[:, :, y0:y1]], axis=-1).astype(BF16)
    w_ab = jnp.pad(w_in[:, :, a0:b1], ((0, 0), (0, 0), (0, AB_COLS - 2 * GDN_HEADS))).astype(BF16)
    w_a16, w_d16, w_r16, w_o16 = (w.astype(BF16) for w in (w_branch_a, w_branch_d, w_branch_r, w_out))
    w_up16, w_down16, w_pg16, w_pp16 = (w.astype(BF16) for w in (w_up, w_down, w_ple_gate, w_ple_proj))
    wa_bd = _block_diag(lru_wa).astype(BF16)
    wx_bd = _block_diag(lru_wx).astype(BF16)
    head_mean = jnp.kron(jnp.eye(MOBA_HEADS, dtype=F32),
                         jnp.full((MOBA_HEAD_DIM, MOBA_HEAD_DIM), 1.0 / MOBA_HEAD_DIM, F32)).astype(BF16)
    gq = _row3(jnp.tile(moba_q_norm, (1, MOBA_HEADS)))
    gk = _row3(jnp.tile(moba_k_norm, (1, MOBA_HEADS)))
    g_mix3, g_ffn3, g_ple3 = _row3(g_mix), _row3(g_ffn), _row3(g_ple)
    a_log3 = _pad_lanes(gdn_a_log, AB_COLS)
    dt_bias3 = _pad_lanes(gdn_dt_bias, AB_COLS)
    out_norm3 = _row3(gdn_out_norm)
    lru_cb3, lru_ba3, lru_bx3, lru_lam3 = _row3(lru_conv_b), _row3(lru_ba), _row3(lru_bx), _row3(lru_lambda)

    cos_p, sin_p = _rope_tables(jnp.arange(seq, dtype=jnp.int32))
    cos_s, sin_s = _rope_tables(past_len + jnp.arange(t_new, dtype=jnp.int32))
    cos_s, sin_s = jnp.tile(cos_s, (bs, 1)), jnp.tile(sin_s, (bs, 1))

    cache_kt = cache_k.transpose(0, 2, 3, 4, 1)
    cache_vt = cache_v.transpose(0, 2, 3, 4, 1)
    pp = p_prompt.reshape(depth * mp, PLE_DIM)
    ps = p_sample.reshape(depth * ms, PLE_DIM)

    zero_gconv = jnp.zeros((bp, 8, GDN_CONV_DIM), F32)
    zero_gstate = jnp.zeros((bp, GDN_HEADS, GDN_DK, GDN_DV), F32)
    zero_lconv = jnp.zeros((bp, 8, LRU_WIDTH), F32)
    zero_lh = jnp.zeros((bp, LRU_WIDTH), F32)
    c_s = 16

    xp = x_prompt.reshape(mp, d)
    xs = x_sample.reshape(ms, d)
    outs = {k: [] for k in ("kp", "vp", "ks", "vs", "gsp", "gss", "gcp", "gcs", "lhp", "lhs", "lcp", "lcs")}

    for l in range(depth):
        u, ab = norm_matmul(xp, g_mix3, w_main, w_ab, l, 1024, 2560)
        q_a, k_a, v_a, kh = moba_prep(u, cos_p, sin_p, head_mean, gq, gk, l, MOBA_BLOCK, seq // MOBA_BLOCK, True)
        tok_p = (bp, seq, MOBA_WIDTH)
        o_a = moba_prompt(q_a.reshape(tok_p), kh, v_a.reshape(tok_p)).reshape(mp, MOBA_WIDTH)
        u3 = u.reshape(bp, seq, U_COLS)
        o_d, gconv, gstate = gdn(u3, ab.reshape(bp, seq, AB_COLS), gdn_conv_w, a_log3, dt_bias3, out_norm3,
                                 zero_gconv, zero_gstate, l, 8, GDN_CHUNK, GDN_CHUNK)
        o_r, lconv, lh = lru_prompt(u3, lru_conv_w, lru_cb3, wa_bd, lru_ba3, wx_bd, lru_bx3, lru_lam3,
                                    zero_lconv, zero_lh, l, 256)
        xp = merge(o_a, o_d.reshape(mp, GDN_WIDTH), o_r.reshape(mp, LRU_WIDTH), u, xp,
                   w_a16, w_d16, w_r16, w_o16, l, 256)
        xp = ffn_ple(xp, g_ffn3, w_up16, w_down16, g_ple3, w_pg16, pp, w_pp16, l, 1024, 1024)
        outs["kp"].append(k_a); outs["vp"].append(v_a); outs["gcp"].append(gconv); outs["gsp"].append(gstate)
        outs["lcp"].append(lconv); outs["lhp"].append(lh)

        u, ab = norm_matmul(xs, g_mix3, w_main, w_ab, l, ms, 1536)
        q_s, k_s, v_s = moba_prep(u, cos_s, sin_s, head_mean, gq, gk, l, ms, 1, False)
        tok_s = (bs, t_new, MOBA_WIDTH)
        o_a = moba_sample(page_table, q_s.reshape(tok_s), k_s.reshape(tok_s), v_s.reshape(tok_s),
                          cache_kt, cache_vt, l)
        o_a = o_a.reshape(ms, MOBA_WIDTH).astype(BF16)
        pad_t = ((0, 0), (0, c_s - t_new), (0, 0))
        u3 = jnp.pad(u.reshape(bs, t_new, U_COLS), pad_t)
        ab3 = jnp.pad(ab.reshape(bs, t_new, AB_COLS), pad_t)
        gbuf = jnp.pad(state_gdn_conv[:, l], ((0, 0), (8 - (CONV_WIDTH - 1), 0), (0, 0)))
        o_d, gconv, gstate = gdn(u3, ab3, gdn_conv_w, a_log3, dt_bias3, out_norm3,
                                 gbuf, state_gdn, l, 8, c_s, t_new)
        o_d = o_d[:, :t_new].reshape(ms, GDN_WIDTH)
        us = u.reshape(bs, t_new, U_COLS)
        x_tm = us[:, :, U_X:U_X + LRU_WIDTH].transpose(1, 0, 2)
        y_tm = us[:, :, U_Y:U_Y + LRU_WIDTH].transpose(1, 0, 2)
        o_r, lconv, lh = lru_sample(x_tm, y_tm, lru_conv_w, lru_cb3, wa_bd, lru_ba3, wx_bd, lru_bx3, lru_lam3,
                                    state_lru_conv[:, l].transpose(1, 0, 2), state_lru_h[:, l], l)
        o_r = o_r.transpose(1, 0, 2).reshape(ms, LRU_WIDTH)
        xs = merge(o_a, o_d, o_r, u, xs, w_a16, w_d16, w_r16, w_o16, l, 256)
        xs = ffn_ple(xs, g_ffn3, w_up16, w_down16, g_ple3, w_pg16, ps, w_pp16, l, ms, 1024)
        outs["ks"].append(k_s); outs["vs"].append(v_s); outs["gcs"].append(gconv); outs["gss"].append(gstate)
        outs["lcs"].append(lconv.transpose(1, 0, 2)); outs["lhs"].append(lh)

    hd = (MOBA_HEADS, MOBA_HEAD_DIM)
    k_prompt = jnp.stack(outs["kp"], axis=1).reshape(bp, seq, depth, *hd)
    v_prompt = jnp.stack(outs["vp"], axis=1).reshape(bp, seq, depth, *hd)
    k_sample = jnp.stack(outs["ks"], axis=1).reshape(bs, t_new, depth, *hd)
    v_sample = jnp.stack(outs["vs"], axis=1).reshape(bs, t_new, depth, *hd)
    return (xp.reshape(bp, seq, d), xs.reshape(bs, t_new, d), k_prompt, v_prompt, k_sample, v_sample,
            jnp.stack(outs["gsp"], axis=1), jnp.stack(outs["gss"], axis=1),
            jnp.stack(outs["gcp"], axis=1), jnp.stack(outs["gcs"], axis=1),
            jnp.stack(outs["lhp"], axis=1), jnp.stack(outs["lhs"], axis=1),
            jnp.stack(outs["lcp"], axis=1), jnp.stack(outs["lcs"], axis=1))
```

```python
import functools
import math

import jax
import jax.numpy as jnp
from jax import lax
from jax.experimental import pallas as pl
from jax.experimental.pallas import tpu as pltpu

F32 = jnp.float32
BF16 = jnp.bfloat16

D_MODEL = 1024
MOBA_HEADS = 8
MOBA_HEAD_DIM = 64
MOBA_WIDTH = MOBA_HEADS * MOBA_HEAD_DIM
MOBA_BLOCK = 256
MOBA_TOPK = 3
ROPE_THETA = 10000.0
PAGE_SIZE = 128
GDN_HEADS = 4
GDN_DK = 128
GDN_DV = 128
GDN_KEY_WIDTH = GDN_HEADS * GDN_DK
GDN_WIDTH = GDN_HEADS * GDN_DV
GDN_CONV_DIM = 2 * GDN_KEY_WIDTH + GDN_WIDTH
GDN_CHUNK = 64
CONV_WIDTH = 4
LRU_WIDTH = 512
LRU_HEADS = 8
LRU_BLOCK = LRU_WIDTH // LRU_HEADS
LRU_C = 8.0
D_FF = 4 * D_MODEL
PLE_DIM = 256
EPS = 1e-6

U_GATES = 0
U_MOBA = 3 * D_MODEL
U_GDN = U_MOBA + 3 * MOBA_WIDTH
U_Z = U_GDN + GDN_CONV_DIM
U_X = U_Z + GDN_WIDTH
U_Y = U_X + LRU_WIDTH
U_COLS = U_Y + LRU_WIDTH
AB_COLS = 128

VMEM_LIMIT = 56 * 1024 * 1024
LANES = 128
LOG2E = math.log2(math.e)
LRU_ROW_PAD = 8


def _params(*sem):
    return pltpu.CompilerParams(dimension_semantics=sem, vmem_limit_bytes=VMEM_LIMIT)


def _dot(a, b):
    return jnp.dot(a, b, preferred_element_type=F32)


def _dot_nt(a, b):
    return lax.dot_general(a, b, (((1,), (1,)), ((), ())), preferred_element_type=F32)


def _dot_tn(a, b):
    return lax.dot_general(a, b, (((0,), (0,)), ((), ())), preferred_element_type=F32)


def _split2(x):
    hi = x.astype(BF16)
    lo = (x - hi.astype(F32)).astype(BF16)
    return hi, lo


def _split3(x):
    hi = x.astype(BF16)
    r = x - hi.astype(F32)
    mid = r.astype(BF16)
    lo = (r - mid.astype(F32)).astype(BF16)
    return hi, mid, lo


def _dot_hp(a, b):
    ah, al = _split2(a)
    bh, bl = _split2(b)
    return _dot(ah, bh) + (_dot(ah, bl) + _dot(al, bh))


def _dot_lhs_exact(m_bf16, x):
    h, m, l = _split3(x)
    return _dot(m_bf16, h) + (_dot(m_bf16, m) + _dot(m_bf16, l))


def _dot_rhs_exact(x, m_bf16):
    h, m, l = _split3(x)
    return _dot(h, m_bf16) + (_dot(m, m_bf16) + _dot(l, m_bf16))


def _rms(xf, gain):
    ms = jnp.mean(xf * xf, axis=-1, keepdims=True)
    return xf * lax.rsqrt(ms + EPS) * gain


def _sigmoid(x):
    return 1.0 / (1.0 + jnp.exp(-x))


def _silu(x):
    return x * _sigmoid(x)


def _softplus(x):
    return jnp.maximum(x, 0.0) + jnp.log1p(jnp.exp(-jnp.abs(x)))


def _expm1(x):
    u = jnp.exp(x)
    um1 = u - 1.0
    lu = jnp.log(u)
    near = um1 * x / jnp.where(lu == 0.0, 1.0, lu)
    near = jnp.where(um1 == 0.0, x, near)
    return jnp.where(jnp.abs(x) < 0.5, near, um1)


def _norm_matmul_kernel(x_ref, g_ref, w_ref, w2_ref, o_ref, o2_ref, h_ref):
    @pl.when(pl.program_id(1) == 0)
    def _():
        h = _rms(x_ref[...], g_ref[...]).astype(BF16)
        h_ref[...] = h
        o2_ref[...] = _dot(h, w2_ref[...])

    o_ref[...] = _dot(h_ref[...], w_ref[...])


def norm_matmul(x, gain, w, w2, layer, tm, tn):
    m, d = x.shape
    n, n2 = w.shape[-1], w2.shape[-1]
    return pl.pallas_call(
        _norm_matmul_kernel,
        out_shape=[jax.ShapeDtypeStruct((m, n), F32), jax.ShapeDtypeStruct((m, n2), F32)],
        grid=(m // tm, n // tn),
        in_specs=[
            pl.BlockSpec((tm, d), lambda i, j: (i, 0)),
            pl.BlockSpec((None, 1, d), lambda i, j: (layer, 0, 0)),
            pl.BlockSpec((None, d, tn), lambda i, j: (layer, 0, j)),
            pl.BlockSpec((None, d, n2), lambda i, j: (layer, 0, 0)),
        ],
        out_specs=[pl.BlockSpec((tm, tn), lambda i, j: (i, j)),
                   pl.BlockSpec((tm, n2), lambda i, j: (i, 0))],
        scratch_shapes=[pltpu.VMEM((tm, d), BF16)],
        compiler_params=_params("parallel", "arbitrary"),
        name="norm_matmul",
    )(x, gain, w, w2)


def _moba_prep_kernel(qkv_ref, cos_ref, sin_ref, bd_ref, gq_ref, gk_ref, q_ref, k_ref, v_ref, *hm_refs):
    cos = cos_ref[...]
    sin = sin_ref[...]
    bd = bd_ref[...]
    lane = lax.broadcasted_iota(jnp.int32, cos.shape, 1)
    first_half = (lane % MOBA_HEAD_DIM) < (MOBA_HEAD_DIM // 2)

    def norm_rot(x, gain):
        ms = _dot_rhs_exact(x * x, bd)
        y = x * lax.rsqrt(ms + EPS) * gain
        partner = jnp.where(first_half,
                            pltpu.roll(y, MOBA_WIDTH - MOBA_HEAD_DIM // 2, 1),
                            pltpu.roll(y, MOBA_HEAD_DIM // 2, 1))
        return y * cos + partner * sin

    q = norm_rot(qkv_ref[:, 0:MOBA_WIDTH], gq_ref[...])
    k = norm_rot(qkv_ref[:, MOBA_WIDTH:2 * MOBA_WIDTH], gk_ref[...])
    v = qkv_ref[:, 2 * MOBA_WIDTH:3 * MOBA_WIDTH]
    q_ref[...] = q
    k_ref[...] = k
    v_ref[...] = v
    if hm_refs:
        kh_ref, = hm_refs
        for h in range(MOBA_HEADS):
            kh_ref[0, h] = k[:, h * MOBA_HEAD_DIM:(h + 1) * MOBA_HEAD_DIM].astype(BF16)


def moba_prep(u, cos, sin, bd, gq, gk, layer, tq, seq_blocks, head_major):
    m = u.shape[0]
    n_steps = m // tq
    out_shape = [jax.ShapeDtypeStruct((m, MOBA_WIDTH), F32)] * 3
    out_specs = [pl.BlockSpec((tq, MOBA_WIDTH), lambda i: (i, 0))] * 3
    if head_major:
        nb = n_steps // seq_blocks
        out_shape.append(jax.ShapeDtypeStruct((nb, MOBA_HEADS, seq_blocks * tq, MOBA_HEAD_DIM), BF16))
        out_specs.append(pl.BlockSpec((1, MOBA_HEADS, tq, MOBA_HEAD_DIM),
                                      lambda i: (i // seq_blocks, 0, i % seq_blocks, 0)))
    tab = pl.BlockSpec((tq, MOBA_WIDTH), lambda i: (i % seq_blocks, 0))
    vec = pl.BlockSpec((None, 1, MOBA_WIDTH), lambda i: (layer, 0, 0))
    return pl.pallas_call(
        _moba_prep_kernel,
        out_shape=out_shape,
        grid=(n_steps,),
        in_specs=[
            pl.BlockSpec((tq, 3 * MOBA_WIDTH), lambda i: (i, U_MOBA // (3 * MOBA_WIDTH))),
            tab, tab,
            pl.BlockSpec((MOBA_WIDTH, MOBA_WIDTH), lambda i: (0, 0)),
            vec, vec,
        ],
        out_specs=out_specs,
        compiler_params=_params("parallel"),
        name="moba_prep",
    )(u, cos, sin, bd, gq, gk)


def _topk_select(gates):
    n = len(gates)
    if n <= MOBA_TOPK:
        return [None] * n
    sel = []
    for a in range(n):
        rank = jnp.zeros(gates[a].shape, F32)
        for b in range(n):
            if b == a:
                continue
            beats = (gates[b] >= gates[a]) if b < a else (gates[b] > gates[a])
            rank = rank + jnp.where(beats, 1.0, 0.0)
        sel.append(rank < float(MOBA_TOPK))
    return sel


HEADS_PER_STEP = LANES // MOBA_HEAD_DIM


def _moba_prompt_kernel(q_ref, k_ref, v_ref, o_ref, *, n_blk):
    blk = MOBA_BLOCK
    dh = MOBA_HEAD_DIM
    scale = dh ** -0.5
    key = lax.broadcasted_iota(jnp.int32, (blk, blk), 0)
    qry = lax.broadcasted_iota(jnp.int32, (blk, blk), 1)
    causal = key <= qry
    vt = [v_ref[0, j * blk:(j + 1) * blk, :].T.astype(BF16) for j in range(n_blk)]
    def scores(i, hh):
        qt = q_ref[0, i * blk:(i + 1) * blk, :].T.astype(BF16)
        return _dot(k_ref[0, hh, 0:(i + 1) * blk, :], qt[hh * dh:(hh + 1) * dh, :])

    def attend(i, hh, s):
        parts = [s[j * blk:(j + 1) * blk, :] for j in range(i + 1)]
        gates = [jnp.sum(parts[j], axis=0, keepdims=True) for j in range(i)]
        sel = _topk_select(gates)
        masked = []
        for j in range(i):
            masked.append(parts[j] if sel[j] is None else jnp.where(sel[j], parts[j], -jnp.inf))
        masked.append(jnp.where(causal, parts[i], -jnp.inf))
        mx = masked[0].max(axis=0, keepdims=True)
        for j in range(1, i + 1):
            mx = jnp.maximum(mx, masked[j].max(axis=0, keepdims=True))
        den = jnp.zeros((1, blk), F32)
        acc = jnp.zeros((dh, blk), F32)
        for j in range(i + 1):
            p = jnp.exp2((masked[j] - mx) * (scale * LOG2E))
            den = den + jnp.sum(p, axis=0, keepdims=True)
            acc = acc + _dot(vt[j][hh * dh:(hh + 1) * dh, :], p.astype(BF16))
        return acc / den

    units = [(i, hh) for i in range(n_blk) for hh in range(HEADS_PER_STEP)]
    s_next = scores(*units[0])
    outs = []
    for n, (i, hh) in enumerate(units):
        s_cur = s_next
        if n + 1 < len(units):
            s_next = scores(*units[n + 1])
        outs.append(attend(i, hh, s_cur))
        if hh == HEADS_PER_STEP - 1:
            o_ref[0, i * blk:(i + 1) * blk, :] = jnp.concatenate(outs, axis=0).T.astype(BF16)
            outs = []


def moba_prompt(q, kh, v):
    nb, nh, t, dh = kh.shape
    tok = pl.BlockSpec((1, t, HEADS_PER_STEP * dh), lambda b, g: (b, 0, g))
    return pl.pallas_call(
        functools.partial(_moba_prompt_kernel, n_blk=t // MOBA_BLOCK),
        out_shape=jax.ShapeDtypeStruct((nb, t, nh * dh), BF16),
        grid=(nb, nh // HEADS_PER_STEP),
        in_specs=[tok, pl.BlockSpec((1, HEADS_PER_STEP, t, dh), lambda b, g: (b, g, 0, 0)), tok],
        out_specs=tok,
        compiler_params=_params("parallel", "parallel"),
        name="moba_prompt",
    )(q, kh, v)


def _moba_sample_kernel(pt_ref, q_ref, kn_ref, vn_ref, *refs, n_pages, t_new):
    del pt_ref
    k_refs, v_refs, o_ref = refs[:n_pages], refs[n_pages:2 * n_pages], refs[2 * n_pages]
    rows = t_new * MOBA_HEADS
    scale = MOBA_HEAD_DIM ** -0.5
    q = q_ref[0]
    head_of_row = lax.broadcasted_iota(jnp.int32, (MOBA_HEADS, MOBA_WIDTH), 0)
    head_of_lane = lax.broadcasted_iota(jnp.int32, (MOBA_HEADS, MOBA_WIDTH), 1) // MOBA_HEAD_DIM
    own_head = head_of_row == head_of_lane
    q_bd = jnp.concatenate(
        [jnp.where(own_head, jnp.broadcast_to(q[t:t + 1, :], (MOBA_HEADS, MOBA_WIDTH)), 0.0) for t in range(t_new)],
        axis=0)
    q16 = q_bd.astype(BF16)
    pages = [_dot(q16, k_refs[j][...].reshape(MOBA_WIDTH, PAGE_SIZE).astype(BF16)) for j in range(n_pages)]

    pages_per_blk = MOBA_BLOCK // PAGE_SIZE
    n_past = n_pages // pages_per_blk
    gates = []
    for n in range(n_past):
        g = jnp.sum(pages[n * pages_per_blk], axis=-1, keepdims=True)
        for r in range(1, pages_per_blk):
            g = g + jnp.sum(pages[n * pages_per_blk + r], axis=-1, keepdims=True)
        gates.append(g)
    sel = _topk_select(gates)
    masked = []
    for j in range(n_pages):
        s_n = sel[j // pages_per_blk]
        masked.append(pages[j] if s_n is None else jnp.where(s_n, pages[j], -jnp.inf))
    tok_of_row = lax.broadcasted_iota(jnp.int32, (rows, 1), 0) // MOBA_HEADS
    kn = kn_ref[0]
    vn = vn_ref[0]
    own = []
    for j in range(t_new):
        s_j = jnp.sum(q_bd * kn[j:j + 1, :], axis=-1, keepdims=True)
        own.append(jnp.where(tok_of_row >= j, s_j, -jnp.inf))
    mx = own[0]
    for j in range(1, t_new):
        mx = jnp.maximum(mx, own[j])
    for j in range(n_pages):
        mx = jnp.maximum(mx, masked[j].max(axis=-1, keepdims=True))
    den = jnp.zeros((rows, 1), F32)
    acc = jnp.zeros((rows, MOBA_WIDTH), F32)
    for j in range(t_new):
        pj = jnp.exp((own[j] - mx) * scale)
        den = den + pj
        acc = acc + pj * vn[j:j + 1, :]
    for j in range(n_pages):
        pj = jnp.exp((masked[j] - mx) * scale)
        den = den + jnp.sum(pj, axis=-1, keepdims=True)
        acc = acc + _dot_nt(pj.astype(BF16), v_refs[j][...].reshape(MOBA_WIDTH, PAGE_SIZE).astype(BF16))
    acc = acc / den
    outs = []
    for t in range(t_new):
        a_t = acc[t * MOBA_HEADS:(t + 1) * MOBA_HEADS, :]
        outs.append(jnp.sum(jnp.where(own_head, a_t, 0.0), axis=0, keepdims=True))
    o_ref[0] = jnp.concatenate(outs, axis=0)


def moba_sample(page_table, q, k_new, v_new, cache_kt, cache_vt, layer):
    nb, t_new, _ = q.shape
    n_pages = page_table.shape[1]
    tok = pl.BlockSpec((1, t_new, MOBA_WIDTH), lambda b, pt: (b, 0, 0))

    def page(j):
        return pl.BlockSpec((None, None, MOBA_HEADS, MOBA_HEAD_DIM, PAGE_SIZE),
                            lambda b, pt: (pt[b * n_pages + j], layer, 0, 0, 0))

    pages = [page(j) for j in range(n_pages)]
    return pl.pallas_call(
        functools.partial(_moba_sample_kernel, n_pages=n_pages, t_new=t_new),
        out_shape=jax.ShapeDtypeStruct((nb, t_new, MOBA_WIDTH), F32),
        grid_spec=pltpu.PrefetchScalarGridSpec(
            num_scalar_prefetch=1,
            grid=(nb,),
            in_specs=[tok, tok, tok] + pages + pages,
            out_specs=tok,
        ),
        compiler_params=_params("parallel"),
        name="moba_sample",
    )(page_table.reshape(-1), q, k_new, v_new, *([cache_kt] * n_pages), *([cache_vt] * n_pages))


def _unit_lower_inverse(lows, c):
    row = lax.broadcasted_iota(jnp.int32, (c, c), 0)
    col = lax.broadcasted_iota(jnp.int32, (c, c), 1)
    eye = jnp.where(row == col, 1.0, 0.0)
    pair = row // 2 == col // 2
    xs = [eye - jnp.where(pair, low, 0.0) for low in lows]
    s = 2
    while s < c:
        sub = (row // (2 * s) == col // (2 * s)) & (row // s != col // s)
        xe = [_dot_hp(x, jnp.where(sub, low, 0.0)) for x, low in zip(xs, lows)]
        xs = [x - _dot_hp(t, x) for x, t in zip(xs, xe)]
        s *= 2
    return xs


def _gdn_kernel(qkv_ref, z_ref, ab_ref, cw_ref, alog_ref, dtb_ref, gn_ref, cbuf_ref, s0_ref,
                o_ref, cnew_ref, snew_ref, xx_ref, st_ref, *, bb, c, t_valid):
    ci = pl.program_id(1)

    @pl.when(ci == 0)
    def _():
        xx_ref[:, 0:8, :] = cbuf_ref[...]
        st_ref[...] = s0_ref[...]

    xx_ref[:, 8:8 + c, :] = qkv_ref[...]
    w = cw_ref[...]
    masked = t_valid < c
    valid = lax.broadcasted_iota(jnp.int32, (c, 1), 0) < t_valid
    row = lax.broadcasted_iota(jnp.int32, (c, c), 0)
    col = lax.broadcasted_iota(jnp.int32, (c, c), 1)
    tri = row >= col
    eye = row == col
    tri16 = jnp.where(tri, 1.0, 0.0).astype(BF16)
    gn = gn_ref[...]

    ys, gcs, betas = [], [], []
    for bi in range(bb):
        y = (w[0:1, :] * xx_ref[bi, 5:5 + c, :] + w[1:2, :] * xx_ref[bi, 6:6 + c, :]
             + w[2:3, :] * xx_ref[bi, 7:7 + c, :] + w[3:4, :] * xx_ref[bi, 8:8 + c, :])
        cnew_ref[bi] = xx_ref[bi, 5 + t_valid:8 + t_valid, :]
        xx_ref[bi, 0:8, :] = xx_ref[bi, c:c + 8, :]
        ys.append(_silu(y))
        ab = ab_ref[bi]
        g_all = -jnp.exp(alog_ref[...]) * _softplus(ab + dtb_ref[...])
        if masked:
            g_all = jnp.where(valid, g_all, 0.0)
        gcs.append(_dot_lhs_exact(tri16, g_all))
        betas.append(_sigmoid(ab))

    chains = [(bi, h) for bi in range(bb) for h in range(GDN_HEADS)]
    q_l, k_l, kb_l, vb_l, gcol_l, glast_l, decay_l = [], [], [], [], [], [], []
    for bi, h in chains:
        lo, hi = h * GDN_DK, (h + 1) * GDN_DK
        y = ys[bi]
        q = y[:, lo:hi]
        k = y[:, GDN_KEY_WIDTH + lo:GDN_KEY_WIDTH + hi]
        v = y[:, 2 * GDN_KEY_WIDTH + lo:2 * GDN_KEY_WIDTH + hi]
        q = q * lax.rsqrt(jnp.sum(q * q, axis=-1, keepdims=True) + EPS) * (GDN_DK ** -0.5)
        k = k * lax.rsqrt(jnp.sum(k * k, axis=-1, keepdims=True) + EPS)
        beta = betas[bi][:, GDN_HEADS + h:GDN_HEADS + h + 1]
        if masked:
            k = jnp.where(valid, k, 0.0)
            v = jnp.where(valid, v, 0.0)
            beta = jnp.where(valid, beta, 0.0)
        g_col = gcs[bi][:, h:h + 1]
        g_row = jnp.sum(jnp.where(eye, g_col, 0.0), axis=0, keepdims=True)
        q_l.append(q)
        k_l.append(k)
        kb_l.append(k * beta)
        vb_l.append(v * beta)
        gcol_l.append(g_col)
        glast_l.append(gcs[bi][c - 1:c, h:h + 1])
        decay_l.append(jnp.exp(jnp.where(tri, g_col - g_row, -jnp.inf)))
    k16_l = [k.astype(BF16) for k in k_l]
    low_l = [jnp.where(row > col, _dot_nt(kb.astype(BF16), k16) * decay, 0.0)
             for kb, k16, decay in zip(kb_l, k16_l, decay_l)]
    a_l = [(_dot_nt(q.astype(BF16), k16) * decay).astype(BF16) for q, k16, decay in zip(q_l, k16_l, decay_l)]
    t_l = [t.astype(BF16) for t in _unit_lower_inverse(low_l, c)]
    eg_l = [jnp.exp(g) for g in gcol_l]
    u_l = [_dot(t, vb.astype(BF16)) for t, vb in zip(t_l, vb_l)]
    wk_l = [_dot(t, (kb * eg).astype(BF16)).astype(BF16) for t, kb, eg in zip(t_l, kb_l, eg_l)]
    qd_l = [(q * eg).astype(BF16) for q, eg in zip(q_l, eg_l)]
    kd_l = [(k * jnp.exp(gl - g)).astype(BF16) for k, gl, g in zip(k_l, glast_l, gcol_l)]
    st_l = [st_ref[bi, h] for bi, h in chains]
    st16_l = [st.astype(BF16) for st in st_l]
    vn_l = [(u - _dot(wk, st16)).astype(BF16) for u, wk, st16 in zip(u_l, wk_l, st16_l)]
    o_l = [_dot(qd, st16) + _dot(a, vn) for qd, st16, a, vn in zip(qd_l, st16_l, a_l, vn_l)]
    sn_l = [st * jnp.exp(gl) + _dot_tn(kd, vn) for st, gl, kd, vn in zip(st_l, glast_l, kd_l, vn_l)]
    for (bi, h), o, sn in zip(chains, o_l, sn_l):
        lo, hi = h * GDN_DK, (h + 1) * GDN_DK
        st_ref[bi, h] = sn
        snew_ref[bi, h] = sn
        on = _rms(o, gn) * _silu(z_ref[bi, :, lo:hi])
        o_ref[bi, :, lo:hi] = on.astype(BF16)


def gdn(u3, ab3, conv_w, a_log, dt_bias, out_norm, conv_buf, s0, layer, bb, c, t_valid, col0=0):
    nb, t, _ = u3.shape
    vec = lambda n: pl.BlockSpec((None, 1, n), lambda b, i: (layer, 0, 0))
    return pl.pallas_call(
        functools.partial(_gdn_kernel, bb=bb, c=c, t_valid=t_valid),
        out_shape=[jax.ShapeDtypeStruct((nb, t, GDN_WIDTH), BF16),
                   jax.ShapeDtypeStruct((nb, CONV_WIDTH - 1, GDN_CONV_DIM), F32),
                   jax.ShapeDtypeStruct((nb, GDN_HEADS, GDN_DK, GDN_DV), F32)],
        grid=(nb // bb, t // c),
        in_specs=[
            pl.BlockSpec((bb, c, GDN_CONV_DIM), lambda b, i: (b, i, (U_GDN - col0) // GDN_CONV_DIM)),
            pl.BlockSpec((bb, c, GDN_WIDTH), lambda b, i: (b, i, (U_Z - col0) // GDN_WIDTH)),
            pl.BlockSpec((bb, c, AB_COLS), lambda b, i: (b, i, 0)),
            pl.BlockSpec((None, CONV_WIDTH, GDN_CONV_DIM), lambda b, i: (layer, 0, 0)),
            vec(AB_COLS), vec(AB_COLS), vec(GDN_DV),
            pl.BlockSpec((bb, 8, GDN_CONV_DIM), lambda b, i: (b, 0, 0)),
            (pl.BlockSpec((bb, GDN_HEADS, GDN_DK, GDN_DV), lambda b, i: (b, 0, 0, 0)) if s0.ndim == 4 else
             pl.BlockSpec((bb, None, GDN_HEADS, GDN_DK, GDN_DV), lambda b, i: (b, layer, 0, 0, 0))),
        ],
        out_specs=[
            pl.BlockSpec((bb, c, GDN_WIDTH), lambda b, i: (b, i, 0)),
            pl.BlockSpec((bb, CONV_WIDTH - 1, GDN_CONV_DIM), lambda b, i: (b, 0, 0)),
            pl.BlockSpec((bb, GDN_HEADS, GDN_DK, GDN_DV), lambda b, i: (b, 0, 0, 0)),
        ],
        scratch_shapes=[pltpu.VMEM((bb, c + 8, GDN_CONV_DIM), F32),
                        pltpu.VMEM((bb, GDN_HEADS, GDN_DK, GDN_DV), F32)],
        compiler_params=_params("parallel", "arbitrary"),
        name="gdn",
    )(u3, u3, ab3, conv_w, a_log, dt_bias, out_norm, conv_buf, s0)


def _lru_gates(xf, wa, ba, wx, bx, lam):
    x16 = xf.astype(BF16)
    r = _sigmoid(_dot(x16, wa) + ba)
    i = _sigmoid(_dot(x16, wx) + bx)
    log_a = -LRU_C * r * _softplus(-lam)
    a = jnp.exp(log_a)
    b = jnp.sqrt(-_expm1(2.0 * log_a)) * (i * xf)
    return a, b


def _lru_prompt_kernel(x_ref, y_ref, cw_ref, cb_ref, wa_ref, ba_ref, wx_ref, bx_ref, lam_ref, cbuf_ref, h0_ref,
                       o_ref, cnew_ref, hlast_ref, xx_ref, a_s, b_s, h_s, hcar, *, nb, tc):
    ci = pl.program_id(0)

    @pl.when(ci == 0)
    def _():
        xx_ref[:, 0:8, :] = cbuf_ref[...]
        hcar[...] = h0_ref[...]

    xx_ref[:, 8:8 + tc, :] = x_ref[...]
    w = cw_ref[...]
    xc = (w[0:1, :] * xx_ref[:, 5:5 + tc, :] + w[1:2, :] * xx_ref[:, 6:6 + tc, :]
          + w[2:3, :] * xx_ref[:, 7:7 + tc, :] + w[3:4, :] * xx_ref[:, 8:8 + tc, :]) + cb_ref[...]
    cnew_ref[...] = xx_ref[:, tc + 5:tc + 8, :]
    xx_ref[:, 0:8, :] = xx_ref[:, tc:tc + 8, :]
    xf = xc.reshape(nb * tc, LRU_WIDTH)
    a, b = _lru_gates(xf, wa_ref[...], ba_ref[...], wx_ref[...], bx_ref[...], lam_ref[...])
    n_lane_tiles = LRU_WIDTH // LANES
    pitch = tc + LRU_ROW_PAD
    for j in range(n_lane_tiles):
        for bi in range(nb):
            a_s[j, bi * pitch:bi * pitch + tc, :] = a[bi * tc:(bi + 1) * tc, j * LANES:(j + 1) * LANES]
            b_s[j, bi * pitch:bi * pitch + tc, :] = b[bi * tc:(bi + 1) * tc, j * LANES:(j + 1) * LANES]

    def step(t, hs):
        new = []
        for j in range(n_lane_tiles):
            h = a_s[j, pl.ds(t, nb, stride=pitch), :] * hs[j] + b_s[j, pl.ds(t, nb, stride=pitch), :]
            h_s[j, pl.ds(t, nb, stride=pitch), :] = h
            new.append(h)
        return tuple(new)

    h0 = hcar[...]
    hs = lax.fori_loop(0, tc, step, tuple(h0[:, j * LANES:(j + 1) * LANES] for j in range(n_lane_tiles)),
                       unroll=8)
    h = jnp.concatenate(hs, axis=1)
    hcar[...] = h
    hlast_ref[...] = h
    h_all = jnp.concatenate(
        [jnp.concatenate([h_s[j, bi * pitch:bi * pitch + tc, :] for bi in range(nb)], axis=0)
         for j in range(n_lane_tiles)], axis=1)
    out = h_all * jax.nn.gelu(y_ref[...].reshape(nb * tc, LRU_WIDTH))
    o_ref[...] = out.reshape(nb, tc, LRU_WIDTH).astype(BF16)


def lru_prompt(u3, conv_w, conv_b, wa, ba, wx, bx, lam, conv_buf, h0, layer, tc):
    nb, t, _ = u3.shape
    vec = pl.BlockSpec((None, 1, LRU_WIDTH), lambda i: (layer, 0, 0))
    mat = pl.BlockSpec((None, LRU_WIDTH, LRU_WIDTH), lambda i: (layer, 0, 0))
    return pl.pallas_call(
        functools.partial(_lru_prompt_kernel, nb=nb, tc=tc),
        out_shape=[jax.ShapeDtypeStruct((nb, t, LRU_WIDTH), BF16),
                   jax.ShapeDtypeStruct((nb, CONV_WIDTH - 1, LRU_WIDTH), F32),
                   jax.ShapeDtypeStruct((nb, LRU_WIDTH), F32)],
        grid=(t // tc,),
        in_specs=[
            pl.BlockSpec((nb, tc, LRU_WIDTH), lambda i: (0, i, U_X // LRU_WIDTH)),
            pl.BlockSpec((nb, tc, LRU_WIDTH), lambda i: (0, i, U_Y // LRU_WIDTH)),
            pl.BlockSpec((None, CONV_WIDTH, LRU_WIDTH), lambda i: (layer, 0, 0)),
            vec, mat, vec, mat, vec, vec,
            pl.BlockSpec((nb, 8, LRU_WIDTH), lambda i: (0, 0, 0)),
            pl.BlockSpec((nb, LRU_WIDTH), lambda i: (0, 0)),
        ],
        out_specs=[
            pl.BlockSpec((nb, tc, LRU_WIDTH), lambda i: (0, i, 0)),
            pl.BlockSpec((nb, CONV_WIDTH - 1, LRU_WIDTH), lambda i: (0, 0, 0)),
            pl.BlockSpec((nb, LRU_WIDTH), lambda i: (0, 0)),
        ],
        scratch_shapes=[pltpu.VMEM((nb, tc + 8, LRU_WIDTH), F32),
                        pltpu.VMEM((LRU_WIDTH // LANES, nb * (tc + LRU_ROW_PAD), LANES), F32),
                        pltpu.VMEM((LRU_WIDTH // LANES, nb * (tc + LRU_ROW_PAD), LANES), F32),
                        pltpu.VMEM((LRU_WIDTH // LANES, nb * (tc + LRU_ROW_PAD), LANES), F32),
                        pltpu.VMEM((nb, LRU_WIDTH), F32)],
        compiler_params=_params("arbitrary"),
        name="lru_prompt",
    )(u3, u3, conv_w, conv_b, wa, ba, wx, bx, lam, conv_buf, h0)


def _lru_sample_kernel(x_ref, y_ref, cw_ref, cb_ref, wa_ref, ba_ref, wx_ref, bx_ref, lam_ref, cbuf_ref, h0_ref,
                       o_ref, cnew_ref, hlast_ref, *, t_new):
    w = cw_ref[...]
    xx = [cbuf_ref[j] for j in range(CONV_WIDTH - 1)] + [x_ref[j] for j in range(t_new)]
    for j in range(CONV_WIDTH - 1):
        cnew_ref[j] = xx[t_new + j]
    h = h0_ref[...]
    for t in range(t_new):
        xf = cb_ref[...] + w[0:1, :] * xx[t]
        for j in range(1, CONV_WIDTH):
            xf = xf + w[j:j + 1, :] * xx[t + j]
        a, b = _lru_gates(xf, wa_ref[...], ba_ref[...], wx_ref[...], bx_ref[...], lam_ref[...])
        h = a * h + b
        o_ref[t] = (h * jax.nn.gelu(y_ref[t])).astype(BF16)
    hlast_ref[...] = h


def lru_sample(x_tm, y_tm, conv_w, conv_b, wa, ba, wx, bx, lam, conv_buf_tm, h0, layer):
    t_new, nb, _ = x_tm.shape
    vec = pl.BlockSpec((None, 1, LRU_WIDTH), lambda i: (layer, 0, 0))
    mat = pl.BlockSpec((None, LRU_WIDTH, LRU_WIDTH), lambda i: (layer, 0, 0))
    tok = pl.BlockSpec((t_new, nb, LRU_WIDTH), lambda i: (0, 0, 0))
    buf = pl.BlockSpec((CONV_WIDTH - 1, nb, LRU_WIDTH), lambda i: (0, 0, 0))
    st = pl.BlockSpec((nb, LRU_WIDTH), lambda i: (0, 0))
    return pl.pallas_call(
        functools.partial(_lru_sample_kernel, t_new=t_new),
        out_shape=[jax.ShapeDtypeStruct((t_new, nb, LRU_WIDTH), BF16),
                   jax.ShapeDtypeStruct((CONV_WIDTH - 1, nb, LRU_WIDTH), F32),
                   jax.ShapeDtypeStruct((nb, LRU_WIDTH), F32)],
        grid=(1,),
        in_specs=[tok, tok, pl.BlockSpec((None, CONV_WIDTH, LRU_WIDTH), lambda i: (layer, 0, 0)),
                  vec, mat, vec, mat, vec, vec, buf, st],
        out_specs=[tok, buf, st],
        compiler_params=_params("arbitrary"),
        name="lru_sample",
    )(x_tm, y_tm, conv_w, conv_b, wa, ba, wx, bx, lam, conv_buf_tm, h0)


def _merge_kernel(oa_ref, od_ref, or_ref, ga_ref, gd_ref, gr_ref, x_ref, wa_ref, wd_ref, wr_ref, wo_ref, o_ref):
    merged = (_sigmoid(ga_ref[...]) * _dot(oa_ref[...], wa_ref[...])
              + _sigmoid(gd_ref[...]) * _dot(od_ref[...], wd_ref[...])
              + _sigmoid(gr_ref[...]) * _dot(or_ref[...], wr_ref[...]))
    o_ref[...] = x_ref[...] + _dot(merged.astype(BF16), wo_ref[...])


def merge(o_a, o_d, o_r, u, x, w_a, w_d, w_r, w_out, layer, tm):
    m, d = x.shape
    br = lambda n: pl.BlockSpec((tm, n), lambda i: (i, 0))
    gate = lambda j: pl.BlockSpec((tm, d), lambda i: (i, j))
    wspec = lambda k: pl.BlockSpec((None, k, d), lambda i: (layer, 0, 0))
    return pl.pallas_call(
        _merge_kernel,
        out_shape=jax.ShapeDtypeStruct((m, d), F32),
        grid=(m // tm,),
        in_specs=[br(MOBA_WIDTH), br(GDN_WIDTH), br(LRU_WIDTH), gate(0), gate(1), gate(2), br(d),
                  wspec(MOBA_WIDTH), wspec(GDN_WIDTH), wspec(LRU_WIDTH), wspec(d)],
        out_specs=br(d),
        compiler_params=_params("parallel"),
        name="merge",
    )(o_a, o_d, o_r, u, u, u, x, w_a, w_d, w_r, w_out)


def _ffn_ple_kernel(x_ref, gf_ref, wu_ref, wd_ref, gp_ref, wg_ref, p_ref, wp_ref, o_ref, h_ref, acc_ref):
    f = pl.program_id(1)

    @pl.when(f == 0)
    def _():
        h_ref[...] = _rms(x_ref[...], gf_ref[...]).astype(BF16)
        acc_ref[...] = jnp.zeros_like(acc_ref)

    a = jnp.maximum(_dot(h_ref[...], wu_ref[...]), 0.0)
    acc_ref[...] += _dot((a * a).astype(BF16), wd_ref[...])

    @pl.when(f == pl.num_programs(1) - 1)
    def _():
        x = x_ref[...] + acc_ref[...]
        gate = _sigmoid(_dot(_rms(x, gp_ref[...]).astype(BF16), wg_ref[...]))
        o_ref[...] = x + gate * _dot(p_ref[...].astype(BF16), wp_ref[...])


def ffn_ple(x, g_ffn, w_up, w_down, g_ple, w_gate, p, w_proj, layer, tm, tf):
    m, d = x.shape
    f = w_up.shape[-1]
    steps = m // tm
    vec = pl.BlockSpec((None, 1, d), lambda i, j: (layer, 0, 0))
    return pl.pallas_call(
        _ffn_ple_kernel,
        out_shape=jax.ShapeDtypeStruct((m, d), F32),
        grid=(steps, f // tf),
        in_specs=[
            pl.BlockSpec((tm, d), lambda i, j: (i, 0)),
            vec,
            pl.BlockSpec((None, d, tf), lambda i, j: (layer, 0, j)),
            pl.BlockSpec((None, tf, d), lambda i, j: (layer, j, 0)),
            vec,
            pl.BlockSpec((None, d, d), lambda i, j: (layer, 0, 0)),
            pl.BlockSpec((tm, PLE_DIM), lambda i, j: (layer * steps + i, 0)),
            pl.BlockSpec((None, PLE_DIM, d), lambda i, j: (layer, 0, 0)),
        ],
        out_specs=pl.BlockSpec((tm, d), lambda i, j: (i, 0)),
        scratch_shapes=[pltpu.VMEM((tm, d), BF16), pltpu.VMEM((tm, d), F32)],
        compiler_params=_params("parallel", "arbitrary"),
        name="ffn_ple",
    )(x, g_ffn, w_up, w_down, g_ple, w_gate, p, w_proj)


def _rope_tables(pos):
    half = MOBA_HEAD_DIM // 2
    inv_freq = ROPE_THETA ** (-jnp.arange(half, dtype=F32) / half)
    ang = pos.astype(F32)[:, None] * inv_freq[None, :]
    cos = jnp.cos(ang)
    sin = jnp.sin(ang)
    cos_h = jnp.concatenate([cos, cos], axis=-1)
    sin_h = jnp.concatenate([-sin, sin], axis=-1)
    return jnp.tile(cos_h, (1, MOBA_HEADS)), jnp.tile(sin_h, (1, MOBA_HEADS))


def _block_diag(w):
    l, h, n, _ = w.shape
    eye = jnp.eye(h, dtype=w.dtype)
    return (w[:, :, :, None, :] * eye[None, :, None, :, None]).reshape(l, h * n, h * n)


def _row3(v):
    return v[:, None, :]


def _pad_lanes(v, n):
    return jnp.pad(v, ((0, 0), (0, n - v.shape[-1])))[:, None, :]


def kernel(x_prompt, x_sample, cache_k, cache_v, state_gdn, state_gdn_conv, state_lru_h, state_lru_conv,
           page_table, p_prompt, p_sample, g_mix, w_in, moba_q_norm, moba_k_norm, w_branch_a, gdn_conv_w,
           gdn_a_log, gdn_dt_bias, gdn_out_norm, w_branch_d, lru_conv_w, lru_conv_b, lru_wa, lru_ba, lru_wx,
           lru_bx, lru_lambda, w_branch_r, w_out, g_ffn, w_up, w_down, g_ple, w_ple_gate, w_ple_proj):
    depth = w_in.shape[0]
    bp, seq, d = x_prompt.shape
    bs, t_new, _ = x_sample.shape
    n_pages = page_table.shape[1]
    past_len = n_pages * PAGE_SIZE
    mp, ms = bp * seq, bs * t_new

    o = 0
    offs = []
    for size in (3 * MOBA_WIDTH, GDN_CONV_DIM, GDN_WIDTH, GDN_HEADS, GDN_HEADS, LRU_WIDTH, LRU_WIDTH, 3 * D_MODEL):
        offs.append((o, o + size))
        o += size
    (m0, m1), (d0, d1), (z0, z1), (a0, a1), (b0, b1), (x0, x1), (y0, y1), (g0, g1) = offs
    w_main = jnp.concatenate([w_in[:, :, g0:g1], w_in[:, :, m0:m1], w_in[:, :, d0:d1], w_in[:, :, z0:z1],
                              w_in[:, :, x0:x1], w_in[:, :, y0:y1]], axis=-1).astype(BF16)
    w_ab = jnp.pad(w_in[:, :, a0:b1], ((0, 0), (0, 0), (0, AB_COLS - 2 * GDN_HEADS))).astype(BF16)
    w_a16, w_d16, w_r16, w_o16 = (w.astype(BF16) for w in (w_branch_a, w_branch_d, w_branch_r, w_out))
    w_up16, w_down16, w_pg16, w_pp16 = (w.astype(BF16) for w in (w_up, w_down, w_ple_gate, w_ple_proj))
    wa_bd = _block_diag(lru_wa).astype(BF16)
    wx_bd = _block_diag(lru_wx).astype(BF16)
    head_mean = jnp.kron(jnp.eye(MOBA_HEADS, dtype=F32),
                         jnp.full((MOBA_HEAD_DIM, MOBA_HEAD_DIM), 1.0 / MOBA_HEAD_DIM, F32)).astype(BF16)
    gq = _row3(jnp.tile(moba_q_norm, (1, MOBA_HEADS)))
    gk = _row3(jnp.tile(moba_k_norm, (1, MOBA_HEADS)))
    g_mix3, g_ffn3, g_ple3 = _row3(g_mix), _row3(g_ffn), _row3(g_ple)
    a_log3 = _pad_lanes(gdn_a_log, AB_COLS)
    dt_bias3 = _pad_lanes(gdn_dt_bias, AB_COLS)
    out_norm3 = _row3(gdn_out_norm)
    lru_cb3, lru_ba3, lru_bx3, lru_lam3 = _row3(lru_conv_b), _row3(lru_ba), _row3(lru_bx), _row3(lru_lambda)

    cos_p, sin_p = _rope_tables(jnp.arange(seq, dtype=jnp.int32))
    cos_s, sin_s = _rope_tables(past_len + jnp.arange(t_new, dtype=jnp.int32))
    cos_s, sin_s = jnp.tile(cos_s, (bs, 1)), jnp.tile(sin_s, (bs, 1))

    cache_kt = cache_k.transpose(0, 2, 3, 4, 1)
    cache_vt = cache_v.transpose(0, 2, 3, 4, 1)
    pp = p_prompt.reshape(depth * mp, PLE_DIM)
    ps = p_sample.reshape(depth * ms, PLE_DIM)

    zero_gconv = jnp.zeros((bp, 8, GDN_CONV_DIM), F32)
    zero_gstate = jnp.zeros((bp, GDN_HEADS, GDN_DK, GDN_DV), F32)
    zero_lconv = jnp.zeros((bp, 8, LRU_WIDTH), F32)
    zero_lh = jnp.zeros((bp, LRU_WIDTH), F32)
    c_s = 16

    xp = x_prompt.reshape(mp, d)
    xs = x_sample.reshape(ms, d)
    outs = {k: [] for k in ("kp", "vp", "ks", "vs", "gsp", "gss", "gcp", "gcs", "lhp", "lhs", "lcp", "lcs")}

    for l in range(depth):
        u, ab = norm_matmul(xp, g_mix3, w_main, w_ab, l, 1024, 2560)
        q_a, k_a, v_a, kh = moba_prep(u, cos_p, sin_p, head_mean, gq, gk, l, MOBA_BLOCK, seq // MOBA_BLOCK, True)
        tok_p = (bp, seq, MOBA_WIDTH)
        o_a = moba_prompt(q_a.reshape(tok_p), kh, v_a.reshape(tok_p)).reshape(mp, MOBA_WIDTH)
        u3 = u.reshape(bp, seq, U_COLS)
        o_d, gconv, gstate = gdn(u3, ab.reshape(bp, seq, AB_COLS), gdn_conv_w, a_log3, dt_bias3, out_norm3,
                                 zero_gconv, zero_gstate, l, 8, GDN_CHUNK, GDN_CHUNK)
        o_r, lconv, lh = lru_prompt(u3, lru_conv_w, lru_cb3, wa_bd, lru_ba3, wx_bd, lru_bx3, lru_lam3,
                                    zero_lconv, zero_lh, l, 256)
        xp = merge(o_a, o_d.reshape(mp, GDN_WIDTH), o_r.reshape(mp, LRU_WIDTH), u, xp,
                   w_a16, w_d16, w_r16, w_o16, l, 512)
        xp = ffn_ple(xp, g_ffn3, w_up16, w_down16, g_ple3, w_pg16, pp, w_pp16, l, 1024, 1024)
        outs["kp"].append(k_a); outs["vp"].append(v_a); outs["gcp"].append(gconv); outs["gsp"].append(gstate)
        outs["lcp"].append(lconv); outs["lhp"].append(lh)

        u, ab = norm_matmul(xs, g_mix3, w_main, w_ab, l, ms, 1536)
        q_s, k_s, v_s = moba_prep(u, cos_s, sin_s, head_mean, gq, gk, l, ms, 1, False)
        tok_s = (bs, t_new, MOBA_WIDTH)
        o_a = moba_sample(page_table, q_s.reshape(tok_s), k_s.reshape(tok_s), v_s.reshape(tok_s),
                          cache_kt, cache_vt, l)
        o_a = o_a.reshape(ms, MOBA_WIDTH).astype(BF16)
        pad_t = ((0, 0), (0, c_s - t_new), (0, 0))
        u3 = jnp.pad(u.reshape(bs, t_new, U_COLS)[:, :, U_GDN:U_X], pad_t)
        ab3 = jnp.pad(ab.reshape(bs, t_new, AB_COLS), pad_t)
        gbuf = jnp.pad(state_gdn_conv[:, l], ((0, 0), (8 - (CONV_WIDTH - 1), 0), (0, 0)))
        o_d, gconv, gstate = gdn(u3, ab3, gdn_conv_w, a_log3, dt_bias3, out_norm3,
                                 gbuf, state_gdn, l, 8, c_s, t_new, col0=U_GDN)
        o_d = o_d[:, :t_new].reshape(ms, GDN_WIDTH)
        us = u.reshape(bs, t_new, U_COLS)
        x_tm = us[:, :, U_X:U_X + LRU_WIDTH].transpose(1, 0, 2)
        y_tm = us[:, :, U_Y:U_Y + LRU_WIDTH].transpose(1, 0, 2)
        o_r, lconv, lh = lru_sample(x_tm, y_tm, lru_conv_w, lru_cb3, wa_bd, lru_ba3, wx_bd, lru_bx3, lru_lam3,
                                    state_lru_conv[:, l].transpose(1, 0, 2), state_lru_h[:, l], l)
        o_r = o_r.transpose(1, 0, 2).reshape(ms, LRU_WIDTH)
        xs = merge(o_a, o_d, o_r, u, xs, w_a16, w_d16, w_r16, w_o16, l, 256)
        xs = ffn_ple(xs, g_ffn3, w_up16, w_down16, g_ple3, w_pg16, ps, w_pp16, l, ms, 1024)
        outs["ks"].append(k_s); outs["vs"].append(v_s); outs["gcs"].append(gconv); outs["gss"].append(gstate)
        outs["lcs"].append(lconv.transpose(1, 0, 2)); outs["lhs"].append(lh)

    hd = (MOBA_HEADS, MOBA_HEAD_DIM)
    k_prompt = jnp.stack(outs["kp"], axis=1).reshape(bp, seq, depth, *hd)
    v_prompt = jnp.stack(outs["vp"], axis=1).reshape(bp, seq, depth, *hd)
    k_sample = jnp.stack(outs["ks"], axis=1).reshape(bs, t_new, depth, *hd)
    v_sample = jnp.stack(outs["vs"], axis=1).reshape(bs, t_new, depth, *hd)
    return (xp.reshape(bp, seq, d), xs.reshape(bs, t_new, d), k_prompt, v_prompt, k_sample, v_sample,
            jnp.stack(outs["gsp"], axis=1), jnp.stack(outs["gss"], axis=1),
            jnp.stack(outs["gcp"], axis=1), jnp.stack(outs["gcs"], axis=1),
            jnp.stack(outs["lhp"], axis=1), jnp.stack(outs["lhs"], axis=1),
            jnp.stack(outs["lcp"], axis=1), jnp.stack(outs["lcs"], axis=1))
```
